```python
import math
import jax
import jax.numpy as jnp
from jax import lax
import numpy as np

D_MODEL = 1024
BATCH = 8
SEQ = 4096
DEPTH = 4
DEC_BATCH = 8
DEC_SEQ = 8192
PAST_LEN = 128

N_MIXERS = 2
N_SSD_LAYERS = (DEPTH + 1) // 2
N_ATTN_LAYERS = DEPTH // 2
EPS = 1e-6
D_FF = 2816
D_INNER = 2 * D_MODEL
SSD_HEAD_DIM = 64
SSD_HEADS = D_INNER // SSD_HEAD_DIM
N_GROUPS = 8
D_STATE = 128
CHUNK = 128
D_CONV = 5
CONV_PAD = D_CONV // 2
GN = N_GROUPS * D_STATE
CONV_DIM = D_INNER + 2 * GN
D_IN_PROJ = D_INNER + CONV_DIM + 2 * SSD_HEADS
HEAD_DIM = 64
N_HEADS = D_MODEL // HEAD_DIM
N_KV_HEADS = 4
GQA_GROUP = N_HEADS // N_KV_HEADS
Q_DIM = N_HEADS * HEAD_DIM
KV_DIM = N_KV_HEADS * HEAD_DIM
QKV_DIM = Q_DIM + 2 * KV_DIM
WINDOW = 128
BLOCK = 128
N_BUCKETS = 32
MAX_DISTANCE = 128

kernel_name = "hybrid_bissd_swa_macaron_encoder"


def rmsnorm(x, g):
    xf = x.astype(jnp.float32)
    y = xf * lax.rsqrt(jnp.mean(xf * xf, axis=-1, keepdims=True) + EPS)
    return (y * g.astype(jnp.float32)).astype(x.dtype)


def swiglu(h, w_gate, w_up, w_down):
    return (jax.nn.silu(h @ w_gate) * (h @ w_up)) @ w_down


def t5_bucket(rel):
    half = N_BUCKETS // 2
    max_exact = half // 2
    ret = jnp.where(rel > 0, half, 0)
    n = jnp.abs(rel)
    large = max_exact + (jnp.log(jnp.maximum(n, 1).astype(jnp.float32) / max_exact)
                         / math.log(MAX_DISTANCE / max_exact) * (half - max_exact)).astype(jnp.int32)
    large = jnp.minimum(large, half - 1)
    return ret + jnp.where(n < max_exact, n, large)


def ssd_chunked_scan(x, dt, A, B, C):
    bsz, s, h, p = x.shape
    g, n = B.shape[-2:]
    r = h // g
    c = s // CHUNK
    xd = (x * dt[..., None]).reshape(bsz, c, CHUNK, g, r, p)
    a = jnp.moveaxis((dt * A).reshape(bsz, c, CHUNK, g, r), 2, -1)
    a_cs = jnp.cumsum(a, axis=-1)
    tril = jnp.tril(jnp.ones((CHUNK, CHUNK), dtype=bool))
    seg = jnp.exp(jnp.where(tril, a_cs[..., :, None] - a_cs[..., None, :], -jnp.inf))
    Bc = B.reshape(bsz, c, CHUNK, g, n)
    Cc = C.reshape(bsz, c, CHUNK, g, n)
    cb = jnp.einsum('bclgn,bcsgn->bcgls', Cc, Bc)
    y_diag = jnp.einsum('bcgls,bcgrls,bcsgrp->bclgrp', cb, seg, xd)
    decay_to_end = jnp.exp(a_cs[..., -1:] - a_cs)
    chunk_states = jnp.einsum('bclgn,bcgrl,bclgrp->bcgrpn', Bc, decay_to_end, xd)
    chunk_decay = jnp.exp(a_cs[..., -1])

    def step(state, inp):
        st, dec = inp
        return state * dec[..., None, None] + st, state

    init = jnp.zeros((bsz, g, r, p, n), jnp.float32)
    _, prev = lax.scan(step, init, (jnp.moveaxis(chunk_states, 1, 0), jnp.moveaxis(chunk_decay, 1, 0)))
    prev = jnp.moveaxis(prev, 0, 1)
    y_off = jnp.einsum('bclgn,bcgrpn,bcgrl->bclgrp', Cc, prev, jnp.exp(a_cs))
    return (y_diag + y_off).reshape(bsz, s, h, p)


def ssd_mixer(h, w_in, conv_w, conv_b, dt_bias, A_log, D, norm_g, w_out):
    bsz, s, _ = h.shape
    proj = h @ w_in
    z = proj[..., :D_INNER]
    xbc = proj[..., D_INNER:D_INNER + CONV_DIM]
    dt_raw = proj[..., D_INNER + CONV_DIM:].astype(jnp.float32)
    xbc = lax.conv_general_dilated(xbc, conv_w[:, None, :], window_strides=(1,),
                                   padding=[(CONV_PAD, CONV_PAD)],
                                   dimension_numbers=('NWC', 'WIO', 'NWC'),
                                   feature_group_count=CONV_DIM)
    xbc = jax.nn.silu(xbc + conv_b).astype(jnp.float32)
    xs = xbc[..., :D_INNER].reshape(bsz, s, SSD_HEADS, SSD_HEAD_DIM)
    Bm = xbc[..., D_INNER:D_INNER + GN].reshape(bsz, s, N_GROUPS, D_STATE)
    Cm = xbc[..., D_INNER + GN:].reshape(bsz, s, N_GROUPS, D_STATE)
    dt = jax.nn.softplus(dt_raw.reshape(bsz, s, 2, SSD_HEADS) + dt_bias.astype(jnp.float32))
    A = -jnp.exp(A_log.astype(jnp.float32))
    flip = lambda t: jnp.flip(t, axis=1)
    y_fwd = ssd_chunked_scan(xs, dt[:, :, 0], A[0], Bm, Cm)
    y_bwd = flip(ssd_chunked_scan(flip(xs), flip(dt[:, :, 1]), A[1], flip(Bm), flip(Cm)))
    y = y_fwd + y_bwd + xs * D.astype(jnp.float32)[:, None]
    y = y.reshape(bsz, s, D_INNER) * jax.nn.silu(z.astype(jnp.float32))
    yg = y.reshape(bsz, s, N_GROUPS, D_INNER // N_GROUPS)
    yg = yg * lax.rsqrt(jnp.mean(yg * yg, axis=-1, keepdims=True) + EPS)
    y = (yg.reshape(bsz, s, D_INNER) * norm_g.astype(jnp.float32)).astype(h.dtype)
    return y @ w_out


def window_attention(h, w_qkv, sink, w_out, rel_bias):
    bsz, s, _ = h.shape
    nb = s // BLOCK
    qkv = h @ w_qkv
    q = qkv[..., :Q_DIM].reshape(bsz, nb, BLOCK, N_KV_HEADS, GQA_GROUP, HEAD_DIM)
    k = qkv[..., Q_DIM:Q_DIM + KV_DIM].reshape(bsz, s, N_KV_HEADS, HEAD_DIM)
    v = qkv[..., Q_DIM + KV_DIM:].reshape(bsz, s, N_KV_HEADS, HEAD_DIM)

    def band(t):
        tp = jnp.pad(t, ((0, 0), (BLOCK, BLOCK), (0, 0), (0, 0)))
        tp = tp.reshape(bsz, nb + 2, BLOCK, N_KV_HEADS, HEAD_DIM)
        return jnp.concatenate([tp[:, :-2], tp[:, 1:-1], tp[:, 2:]], axis=2)

    kb, vb = band(k), band(v)
    scores = jnp.einsum('bnqkgd,bnskd->bnkgqs', q, kb,
                        preferred_element_type=jnp.float32) * (HEAD_DIM ** -0.5)
    qi = jnp.arange(BLOCK)[:, None]
    kj = jnp.arange(3 * BLOCK)[None, :]
    rel = kj - BLOCK - qi
    bias = rel_bias[t5_bucket(rel)].astype(jnp.float32)
    bias = jnp.transpose(bias, (2, 0, 1)).reshape(N_KV_HEADS, GQA_GROUP, BLOCK, 3 * BLOCK)
    kpos = (jnp.arange(nb)[:, None] - 1) * BLOCK + kj
    mask = (jnp.abs(rel) <= WINDOW)[None] & ((kpos >= 0) & (kpos < s))[:, None, :]
    scores = jnp.where(mask[None, :, None, None], scores + bias, -jnp.inf)
    sink_b = sink.astype(jnp.float32).reshape(N_KV_HEADS, GQA_GROUP, 1, 1)
    m = jnp.maximum(jnp.max(scores, axis=-1, keepdims=True), sink_b)
    e = jnp.exp(scores - m)
    probs = e / (jnp.sum(e, axis=-1, keepdims=True) + jnp.exp(sink_b - m))
    o = jnp.einsum('bnkgqs,bnskd->bnqkgd', probs.astype(vb.dtype), vb).reshape(bsz, s, Q_DIM)
    return o @ w_out


def encoder(x, norm_g, ffn_w_gate, ffn_w_up, ffn_w_down, ssd_w_in, ssd_conv_w, ssd_conv_b,
            ssd_dt_bias, ssd_A_log, ssd_D, ssd_norm_g, ssd_w_out, attn_w_qkv, attn_sink,
            attn_w_out, rel_bias):
    for i in range(DEPTH):
        ng = norm_g[i]
        x = x + 0.5 * rmsnorm(swiglu(rmsnorm(x, ng[0]), ffn_w_gate[i, 0], ffn_w_up[i, 0],
                                     ffn_w_down[i, 0]), ng[1])
        h = rmsnorm(x, ng[2])
        j = i // N_MIXERS
        if i % N_MIXERS == 0:
            mix = ssd_mixer(h, ssd_w_in[j], ssd_conv_w[j], ssd_conv_b[j], ssd_dt_bias[j],
                            ssd_A_log[j], ssd_D[j], ssd_norm_g[j], ssd_w_out[j])
        else:
            mix = window_attention(h, attn_w_qkv[j], attn_sink[j], attn_w_out[j], rel_bias)
        x = x + rmsnorm(mix, ng[3])
        x = x + 0.5 * rmsnorm(swiglu(rmsnorm(x, ng[4]), ffn_w_gate[i, 1], ffn_w_up[i, 1],
                                     ffn_w_down[i, 1]), ng[5])
    return x


def setup_inputs(seed: int = 0) -> dict:
    key = jax.random.key(seed)
    ks = jax.random.split(key, 20)
    f32 = jnp.float32

    def nrm(k, shape, fan):
        return jax.random.normal(k, shape, f32) * (fan ** -0.5)

    dt0 = jnp.exp(jax.random.uniform(ks[9], (N_SSD_LAYERS, 2, SSD_HEADS), f32,
                                     math.log(1e-3), math.log(1e-1)))
    return {
        "x_prompt": jax.random.normal(ks[0], (BATCH, SEQ, D_MODEL), f32),
        "x_sample": jax.random.normal(ks[1], (DEC_BATCH, DEC_SEQ, D_MODEL), f32),
        "norm_g": 1.0 + 0.05 * jax.random.normal(ks[2], (DEPTH, 6, D_MODEL), f32),
        "ffn_w_gate": nrm(ks[3], (DEPTH, 2, D_MODEL, D_FF), D_MODEL),
        "ffn_w_up": nrm(ks[4], (DEPTH, 2, D_MODEL, D_FF), D_MODEL),
        "ffn_w_down": nrm(ks[5], (DEPTH, 2, D_FF, D_MODEL), D_FF),
        "ssd_w_in": nrm(ks[6], (N_SSD_LAYERS, D_MODEL, D_IN_PROJ), D_MODEL),
        "ssd_conv_w": nrm(ks[7], (N_SSD_LAYERS, D_CONV, CONV_DIM), D_CONV),
        "ssd_conv_b": 0.02 * jax.random.normal(ks[8], (N_SSD_LAYERS, CONV_DIM), f32),
        "ssd_dt_bias": dt0 + jnp.log(-jnp.expm1(-dt0)),
        "ssd_A_log": jnp.log(jax.random.uniform(ks[10], (N_SSD_LAYERS, 2, SSD_HEADS), f32, 1.0, 16.0)),
        "ssd_D": 1.0 + 0.1 * jax.random.normal(ks[11], (N_SSD_LAYERS, SSD_HEADS), f32),
        "ssd_norm_g": 1.0 + 0.05 * jax.random.normal(ks[12], (N_SSD_LAYERS, D_INNER), f32),
        "ssd_w_out": nrm(ks[13], (N_SSD_LAYERS, D_INNER, D_MODEL), D_INNER),
        "attn_w_qkv": nrm(ks[14], (N_ATTN_LAYERS, D_MODEL, QKV_DIM), D_MODEL),
        "attn_sink": 0.5 * jax.random.normal(ks[15], (N_ATTN_LAYERS, N_HEADS), f32),
        "attn_w_out": nrm(ks[16], (N_ATTN_LAYERS, Q_DIM, D_MODEL), Q_DIM),
        "rel_bias": 0.5 * jax.random.normal(ks[17], (N_BUCKETS, N_HEADS), f32),
    }


def reference(x_prompt, x_sample, norm_g, ffn_w_gate, ffn_w_up, ffn_w_down, ssd_w_in, ssd_conv_w,
              ssd_conv_b, ssd_dt_bias, ssd_A_log, ssd_D, ssd_norm_g, ssd_w_out, attn_w_qkv,
              attn_sink, attn_w_out, rel_bias):
    y_prompt = encoder(x_prompt, norm_g, ffn_w_gate, ffn_w_up, ffn_w_down, ssd_w_in, ssd_conv_w,
                       ssd_conv_b, ssd_dt_bias, ssd_A_log, ssd_D, ssd_norm_g, ssd_w_out,
                       attn_w_qkv, attn_sink, attn_w_out, rel_bias)
    y_sample = encoder(x_sample, norm_g, ffn_w_gate, ffn_w_up, ffn_w_down, ssd_w_in, ssd_conv_w,
                       ssd_conv_b, ssd_dt_bias, ssd_A_log, ssd_D, ssd_norm_g, ssd_w_out,
                       attn_w_qkv, attn_sink, attn_w_out, rel_bias)
    return (y_prompt, y_sample)
```

```python
import functools

import jax
import jax.numpy as jnp
from jax import lax
from jax.experimental import pallas as pl
from jax.experimental.pallas import tpu as pltpu

F32 = jnp.float32
MXU_DTYPE = jnp.bfloat16

EPS = 1e-6
SSD_HEAD_DIM = 64
N_GROUPS = 8
D_STATE = 128
CHUNK = 128
D_CONV = 5
CONV_PAD = D_CONV // 2
HALO = 8
HEAD_DIM = 64
N_KV_HEADS = 4
BLOCK = 128
WINDOW = 128
N_BUCKETS = 32

V7X_VMEM_BYTES = 64 * 1024 * 1024
VMEM_LIMIT_BYTES = V7X_VMEM_BYTES - 8 * 1024 * 1024

NEG_INF = float("-inf")


def _rmsnorm(x, g):
    return x * lax.rsqrt(jnp.mean(x * x, axis=-1, keepdims=True) + EPS) * g


def _silu(x):
    return x * jax.nn.sigmoid(x)


def _mm(a, b):
    return jnp.dot(a.astype(MXU_DTYPE), b.astype(MXU_DTYPE), preferred_element_type=F32)


def _mm_nt(a, b):
    return lax.dot_general(a.astype(MXU_DTYPE), b.astype(MXU_DTYPE), (((1,), (1,)), ((), ())),
                           preferred_element_type=F32)


def _mm_f32(a, b):
    return jnp.dot(a, b, precision=lax.Precision.HIGHEST, preferred_element_type=F32)


def _const_spec(shape):
    zeros = (0,) * len(shape)
    return pl.BlockSpec(shape, lambda *_: zeros, pipeline_mode=pl.Buffered(1))


def _params(n_axes):
    return pltpu.CompilerParams(dimension_semantics=("arbitrary",) * n_axes,
                                vmem_limit_bytes=VMEM_LIMIT_BYTES)


def _ffn_body(x_ref, gpre_ref, gpost_ref, wg_ref, wu_ref, wd_ref, o_ref):
    x = x_ref[...]
    h = _rmsnorm(x, gpre_ref[...]).astype(MXU_DTYPE)
    g = jnp.dot(h, wg_ref[...], preferred_element_type=F32)
    u = jnp.dot(h, wu_ref[...], preferred_element_type=F32)
    a = (_silu(g) * u).astype(MXU_DTYPE)
    y = jnp.dot(a, wd_ref[...], preferred_element_type=F32)
    o_ref[...] = x + 0.5 * _rmsnorm(y, gpost_ref[...])


def _ffn(x, g_pre, g_post, wg, wu, wd, tm):
    m, d = x.shape
    f = wg.shape[1]
    assert m % tm == 0
    row = pl.BlockSpec((tm, d), lambda i: (i, 0))
    return pl.pallas_call(
        _ffn_body,
        grid=(m // tm,),
        in_specs=[row, _const_spec((1, d)), _const_spec((1, d)),
                  _const_spec((d, f)), _const_spec((d, f)), _const_spec((f, d))],
        out_specs=row,
        out_shape=jax.ShapeDtypeStruct((m, d), x.dtype),
        compiler_params=_params(1),
        name="ffn",
    )(x, g_pre, g_post, wg, wu, wd)


def _norm_proj_body(n_w, n_wt, x_ref, g_ref, *refs):
    w_refs = refs[:n_w]
    wt_refs = refs[n_w:n_w + n_wt]
    o_refs = refs[n_w + n_wt:]
    h = _rmsnorm(x_ref[...], g_ref[...]).astype(MXU_DTYPE)
    for w_ref, o_ref in zip(w_refs, o_refs[:n_w]):
        o_ref[...] = jnp.dot(h, w_ref[...], preferred_element_type=F32).astype(o_ref.dtype)
    for wt_ref, o_ref in zip(wt_refs, o_refs[n_w:]):
        o_ref[...] = _mm_nt(wt_ref[...], h).astype(o_ref.dtype)


def _norm_proj(x, g, ws, wts, tm):
    m, d = x.shape
    assert m % tm == 0
    row = lambda n: pl.BlockSpec((tm, n), lambda i: (i, 0))
    col = lambda n: pl.BlockSpec((n, tm), lambda i: (0, i))
    return pl.pallas_call(
        functools.partial(_norm_proj_body, len(ws), len(wts)),
        grid=(m // tm,),
        in_specs=[row(d), _const_spec((1, d))] + [_const_spec(w.shape) for w in ws]
                 + [_const_spec(w.shape) for w in wts],
        out_specs=[row(w.shape[1]) for w in ws] + [col(w.shape[0]) for w in wts],
        out_shape=[jax.ShapeDtypeStruct((m, w.shape[1]), F32) for w in ws]
                  + [jax.ShapeDtypeStruct((w.shape[0], m), F32) for w in wts],
        compiler_params=_params(1),
        name="norm_proj",
    )(x, g, *ws, *wts)


def _scan_chunk(xs, bm, cm, dt, dt_t, alog, alog_t, state_ref, y_ref, reverse):
    length = xs.shape[0]
    n_heads = dt.shape[1]
    heads_per_group = n_heads // N_GROUPS
    row = lax.broadcasted_iota(jnp.int32, (length, length), 0)
    col = lax.broadcasted_iota(jnp.int32, (length, length), 1)
    valid = (col >= row) if reverse else (col <= row)
    valid_t = (col <= row) if reverse else (col >= row)
    a = dt * (-jnp.exp(alog))
    a_t = dt_t * (-jnp.exp(alog_t))
    acs = _mm_f32(valid.astype(F32), a)
    acs_t = _mm_f32(a_t, valid_t.astype(F32))
    tot_t = _mm_f32(a_t, jnp.ones((length, length), F32))
    end = 0 if reverse else length - 1
    w_t = dt_t * jnp.exp(tot_t - acs_t)
    off = jnp.exp(acs)
    chunk_decay = jnp.exp(acs[end:end + 1, :])

    for g in range(N_GROUPS):
        bg = bm[:, g * D_STATE:(g + 1) * D_STATE]
        cg = cm[:, g * D_STATE:(g + 1) * D_STATE].astype(MXU_DTYPE)
        cb = _mm_nt(cg, bg)
        bg_t = bg.T
        for r in range(heads_per_group):
            h = g * heads_per_group + r
            lanes = slice(h * SSD_HEAD_DIM, (h + 1) * SSD_HEAD_DIM)
            diff = acs[:, h:h + 1] - acs_t[h:h + 1, :]
            seg = jnp.exp(jnp.where(valid, diff, NEG_INF))
            m_h = cb * seg * dt_t[h:h + 1, :]
            x_h = xs[:, lanes].astype(MXU_DTYPE)
            y_diag = _mm(m_h, x_h)
            prev = state_ref[h]
            y_off = _mm(cg, prev) * off[:, h:h + 1]
            y_ref[:, lanes] = y_diag + y_off
            new = _mm(bg_t * w_t[h:h + 1, :], x_h)
            state_ref[h] = prev * chunk_decay[:, h:h + 1] + new


def _ssd_bwd_body(n_heads, xbc_ref, prev_ref, next_ref, dt_ref, dtt_ref, convw_ref, convb_ref,
                  bias_ref, biast_ref, alog_ref, alogt_ref,
                  act_ref, dtp_ref, dtpt_ref, y_ref, ext_ref, state_ref):
    i = pl.program_id(1)
    n_chunks = pl.num_programs(1)
    chunk = n_chunks - 1 - i
    length = xbc_ref.shape[0]
    d_inner = n_heads * SSD_HEAD_DIM
    gn = N_GROUPS * D_STATE

    @pl.when(i == 0)
    def _():
        state_ref[...] = jnp.zeros_like(state_ref)

    ext_ref[0:HALO, :] = jnp.where(chunk > 0, prev_ref[...], 0.0)
    ext_ref[HALO:HALO + length, :] = xbc_ref[...]
    ext_ref[HALO + length:, :] = jnp.where(chunk < n_chunks - 1, next_ref[...], 0.0)
    acc = convb_ref[...] + convw_ref[0:1, :] * ext_ref[pl.ds(HALO - CONV_PAD, length), :]
    for k in range(1, D_CONV):
        acc = acc + convw_ref[k:k + 1, :] * ext_ref[pl.ds(HALO - CONV_PAD + k, length), :]
    act = _silu(acc)
    act_ref[...] = act

    dtp = jax.nn.softplus(dt_ref[...] + bias_ref[...])
    dtp_t = jax.nn.softplus(dtt_ref[...] + biast_ref[...])
    dtp_ref[...] = dtp
    dtpt_ref[...] = dtp_t

    _scan_chunk(act[:, :d_inner], act[:, d_inner:d_inner + gn], act[:, d_inner + gn:],
                dtp[:, n_heads:], dtp_t[n_heads:, :], alog_ref[:, n_heads:], alogt_ref[n_heads:, :],
                state_ref, y_ref, reverse=True)


def _ssd_bwd(xbc, dt_raw, dt_raw_t, conv_w, conv_b, dt_bias, dt_bias_t, alog, alog_t, batch, n_heads):
    m, c = xbc.shape
    seq = m // batch
    assert seq % CHUNK == 0
    nc = seq // CHUNK
    hb = CHUNK // HALO
    d_inner = n_heads * SSD_HEAD_DIM
    rev = lambda b, i: b * nc + (nc - 1 - i)
    row = lambda n: pl.BlockSpec((CHUNK, n), lambda b, i: (rev(b, i), 0))
    col = lambda n: pl.BlockSpec((n, CHUNK), lambda b, i: (0, rev(b, i)))
    prev = pl.BlockSpec((HALO, c), lambda b, i: (jnp.maximum(rev(b, i) * hb - 1, 0), 0))
    nxt = pl.BlockSpec((HALO, c), lambda b, i: (jnp.minimum((rev(b, i) + 1) * hb, m // HALO - 1), 0))
    return pl.pallas_call(
        functools.partial(_ssd_bwd_body, n_heads),
        grid=(batch, nc),
        in_specs=[row(c), prev, nxt, row(2 * n_heads), col(2 * n_heads),
                  _const_spec(conv_w.shape), _const_spec(conv_b.shape),
                  _const_spec(dt_bias.shape), _const_spec(dt_bias_t.shape),
                  _const_spec(alog.shape), _const_spec(alog_t.shape)],
        out_specs=[row(c), row(2 * n_heads), col(2 * n_heads), row(d_inner)],
        out_shape=[jax.ShapeDtypeStruct((m, c), F32),
                   jax.ShapeDtypeStruct((m, 2 * n_heads), F32),
                   jax.ShapeDtypeStruct((2 * n_heads, m), F32),
                   jax.ShapeDtypeStruct((m, d_inner), F32)],
        scratch_shapes=[pltpu.VMEM((CHUNK + 2 * HALO, c), F32),
                        pltpu.VMEM((n_heads, D_STATE, SSD_HEAD_DIM), F32)],
        compiler_params=_params(2),
        name="ssd_bwd",
    )(xbc, xbc, xbc, dt_raw, dt_raw_t, conv_w, conv_b, dt_bias, dt_bias_t, alog, alog_t)


def _ssd_fwd_body(n_heads, act_ref, dtp_ref, dtpt_ref, ybwd_ref, z_ref, x_ref, alog_ref, alogt_ref,
                  dskip_ref, normg_ref, wout_ref, gpost_ref, o_ref, y_ref, state_ref):
    d_inner = n_heads * SSD_HEAD_DIM
    gn = N_GROUPS * D_STATE
    group_width = d_inner // N_GROUPS

    @pl.when(pl.program_id(1) == 0)
    def _():
        state_ref[...] = jnp.zeros_like(state_ref)

    xs = act_ref[:, :d_inner]
    _scan_chunk(xs, act_ref[:, d_inner:d_inner + gn], act_ref[:, d_inner + gn:],
                dtp_ref[:, :n_heads], dtpt_ref[:n_heads, :], alog_ref[:, :n_heads], alogt_ref[:n_heads, :],
                state_ref, y_ref, reverse=False)

    y = (y_ref[...] + ybwd_ref[...] + xs * dskip_ref[...]) * _silu(z_ref[...])
    for g in range(N_GROUPS):
        lanes = slice(g * group_width, (g + 1) * group_width)
        yg = y[:, lanes]
        yg = yg * lax.rsqrt(jnp.mean(yg * yg, axis=-1, keepdims=True) + EPS)
        y_ref[:, lanes] = yg * normg_ref[:, lanes]
    mix = jnp.dot(y_ref[...].astype(MXU_DTYPE), wout_ref[...], preferred_element_type=F32)
    o_ref[...] = x_ref[...] + _rmsnorm(mix, gpost_ref[...])


def _ssd_fwd(act, dtp, dtp_t, y_bwd, z, x, alog, alog_t, d_skip, norm_g, w_out, g_post, batch, n_heads):
    m, c = act.shape
    d = x.shape[1]
    seq = m // batch
    nc = seq // CHUNK
    d_inner = n_heads * SSD_HEAD_DIM
    row = lambda n: pl.BlockSpec((CHUNK, n), lambda b, i: (b * nc + i, 0))
    col = lambda n: pl.BlockSpec((n, CHUNK), lambda b, i: (0, b * nc + i))
    return pl.pallas_call(
        functools.partial(_ssd_fwd_body, n_heads),
        grid=(batch, nc),
        in_specs=[row(c), row(2 * n_heads), col(2 * n_heads), row(d_inner), row(d_inner), row(d),
                  _const_spec(alog.shape), _const_spec(alog_t.shape), _const_spec(d_skip.shape),
                  _const_spec(norm_g.shape), _const_spec(w_out.shape), _const_spec(g_post.shape)],
        out_specs=row(d),
        out_shape=jax.ShapeDtypeStruct((m, d), F32),
        scratch_shapes=[pltpu.VMEM((CHUNK, d_inner), F32),
                        pltpu.VMEM((n_heads, D_STATE, SSD_HEAD_DIM), F32)],
        compiler_params=_params(2),
        name="ssd_fwd",
    )(act, dtp, dtp_t, y_bwd, z, x, alog, alog_t, d_skip, norm_g, w_out, g_post)


def _t5_bias_table(relb_ref, bias_ref):
    n_heads = bias_ref.shape[0]
    qi = lax.broadcasted_iota(jnp.int32, (BLOCK, 3 * BLOCK), 0)
    kj = lax.broadcasted_iota(jnp.int32, (BLOCK, 3 * BLOCK), 1)
    rel = kj - BLOCK - qi
    n = jnp.abs(rel)
    half = N_BUCKETS // 2
    max_exact = half // 2
    large = jnp.full_like(n, max_exact)
    for k in range(1, half - max_exact):
        large = large + (n * n >= (max_exact * max_exact) << k).astype(jnp.int32)
    bucket = jnp.where(rel > 0, half, 0) + jnp.where(n < max_exact, n, large)
    for h in range(n_heads):
        acc = jnp.zeros((BLOCK, 3 * BLOCK), F32)
        for b in range(N_BUCKETS):
            acc = jnp.where(bucket == b, relb_ref[b, h], acc)
        bias_ref[h] = acc


def _attn_body(q_ref, kp_ref, kc_ref, kn_ref, vp_ref, vc_ref, vn_ref, x_ref, sink_ref, relb_ref,
               wout_ref, gpost_ref, o_ref, bias_ref, ctx_ref):
    blk = pl.program_id(1)
    n_blocks = pl.num_programs(1)
    n_heads = bias_ref.shape[0]
    group = n_heads // N_KV_HEADS

    @pl.when((pl.program_id(0) == 0) & (blk == 0))
    def _():
        _t5_bias_table(relb_ref, bias_ref)

    qi = lax.broadcasted_iota(jnp.int32, (BLOCK, 3 * BLOCK), 0)
    kj = lax.broadcasted_iota(jnp.int32, (BLOCK, 3 * BLOCK), 1)
    rel = kj - BLOCK - qi
    in_seq = ((kj >= BLOCK) | (blk > 0)) & ((kj < 2 * BLOCK) | (blk < n_blocks - 1))
    mask = (jnp.abs(rel) <= WINDOW) & in_seq

    kb = jnp.concatenate([kp_ref[...], kc_ref[...], kn_ref[...]], axis=0).astype(MXU_DTYPE)
    vb = jnp.concatenate([vp_ref[...], vc_ref[...], vn_ref[...]], axis=0).astype(MXU_DTYPE)
    for kv in range(N_KV_HEADS):
        kv_lanes = slice(kv * HEAD_DIM, (kv + 1) * HEAD_DIM)
        k_h = kb[:, kv_lanes]
        v_h = vb[:, kv_lanes]
        for r in range(group):
            h = kv * group + r
            lanes = slice(h * HEAD_DIM, (h + 1) * HEAD_DIM)
            scores = _mm_nt(q_ref[:, lanes], k_h) * (HEAD_DIM ** -0.5)
            scores = jnp.where(mask, scores + bias_ref[h], NEG_INF)
            sink = sink_ref[h]
            mx = jnp.maximum(jnp.max(scores, axis=-1, keepdims=True), sink)
            e = jnp.exp(scores - mx)
            denom = jnp.sum(e, axis=-1, keepdims=True) + jnp.exp(sink - mx)
            probs = e / denom
            ctx_ref[:, lanes] = _mm(probs, v_h)
    mix = jnp.dot(ctx_ref[...].astype(MXU_DTYPE), wout_ref[...], preferred_element_type=F32)
    o_ref[...] = x_ref[...] + _rmsnorm(mix, gpost_ref[...])


def _attn(q, k, v, x, sink, rel_bias, w_out, g_post, batch):
    m, qd = q.shape
    kvd = k.shape[1]
    d = x.shape[1]
    n_heads = qd // HEAD_DIM
    assert (m // batch) % BLOCK == 0
    nb = m // batch // BLOCK
    cur = lambda n: pl.BlockSpec((BLOCK, n), lambda b, i: (b * nb + i, 0))
    prev = lambda n: pl.BlockSpec((BLOCK, n), lambda b, i: (b * nb + jnp.maximum(i - 1, 0), 0))
    nxt = lambda n: pl.BlockSpec((BLOCK, n), lambda b, i: (b * nb + jnp.minimum(i + 1, nb - 1), 0))
    smem = pl.BlockSpec(memory_space=pltpu.SMEM)
    return pl.pallas_call(
        _attn_body,
        grid=(batch, nb),
        in_specs=[cur(qd), prev(kvd), cur(kvd), nxt(kvd), prev(kvd), cur(kvd), nxt(kvd), cur(d),
                  smem, smem, _const_spec(w_out.shape), _const_spec(g_post.shape)],
        out_specs=cur(d),
        out_shape=jax.ShapeDtypeStruct((m, d), F32),
        scratch_shapes=[pltpu.VMEM((n_heads, BLOCK, 3 * BLOCK), F32),
                        pltpu.VMEM((BLOCK, qd), F32)],
        compiler_params=_params(2),
        name="attn",
    )(q, k, k, k, v, v, v, x, sink, rel_bias, w_out, g_post)


def _row(v):
    return v.reshape(1, -1)


def _lane_bcast(r):
    return jnp.broadcast_to(r.reshape(-1, 1), (r.shape[1], CHUNK))


def _prep_ssd(w_in, conv_w, conv_b, dt_bias, a_log, d_skip, norm_g, w_out):
    n_heads = d_skip.shape[0]
    d_inner = n_heads * SSD_HEAD_DIM
    conv_dim = conv_w.shape[1]
    w_dt = w_in[:, d_inner + conv_dim:].astype(MXU_DTYPE)
    bias = dt_bias.reshape(1, -1)
    alog = a_log.reshape(1, -1)
    return dict(n_heads=n_heads, w_z=w_in[:, :d_inner].astype(MXU_DTYPE),
                w_xbc=w_in[:, d_inner:d_inner + conv_dim].astype(MXU_DTYPE), w_dt=w_dt, w_dt_t=w_dt.T,
                conv_w=conv_w, conv_b=_row(conv_b), bias=bias, bias_t=_lane_bcast(bias),
                alog=alog, alog_t=_lane_bcast(alog), d_skip=_row(jnp.repeat(d_skip, SSD_HEAD_DIM)),
                norm_g=_row(norm_g), w_out=w_out.astype(MXU_DTYPE))


def _ssd_layer(x, batch, g_pre, g_post, p, tm):
    z, xbc, dt_raw, dt_raw_t = _norm_proj(x, g_pre, [p["w_z"], p["w_xbc"], p["w_dt"]], [p["w_dt_t"]], tm)
    act, dtp, dtp_t, y_bwd = _ssd_bwd(xbc, dt_raw, dt_raw_t, p["conv_w"], p["conv_b"], p["bias"], p["bias_t"],
                                      p["alog"], p["alog_t"], batch, p["n_heads"])
    return _ssd_fwd(act, dtp, dtp_t, y_bwd, z, x, p["alog"], p["alog_t"], p["d_skip"], p["norm_g"],
                    p["w_out"], g_post, batch, p["n_heads"])


def _prep_attn(w_qkv, sink, w_out):
    q_dim = w_out.shape[0]
    kv_dim = (w_qkv.shape[1] - q_dim) // 2
    return dict(w_q=w_qkv[:, :q_dim].astype(MXU_DTYPE), w_k=w_qkv[:, q_dim:q_dim + kv_dim].astype(MXU_DTYPE),
                w_v=w_qkv[:, q_dim + kv_dim:].astype(MXU_DTYPE), sink=sink, w_out=w_out.astype(MXU_DTYPE))


def _attn_layer(x, batch, g_pre, g_post, p, rel_bias, tm):
    q, k, v = _norm_proj(x, g_pre, [p["w_q"], p["w_k"], p["w_v"]], [], tm)
    return _attn(q, k, v, x, p["sink"], rel_bias, p["w_out"], g_post, batch)


def _encoder(x3, norm_g, ffn, mixers, rel_bias, tm):
    batch, seq, d = x3.shape
    x = x3.reshape(batch * seq, d)
    for i, mixer in enumerate(mixers):
        ng = [_row(norm_g[i, s]) for s in range(norm_g.shape[1])]
        x = _ffn(x, ng[0], ng[1], *ffn[i][0], tm)
        if "w_z" in mixer:
            x = _ssd_layer(x, batch, ng[2], ng[3], mixer, tm)
        else:
            x = _attn_layer(x, batch, ng[2], ng[3], mixer, rel_bias, tm)
        x = _ffn(x, ng[4], ng[5], *ffn[i][1], tm)
    return x.reshape(batch, seq, d)


TOKEN_TILE = 512


def kernel(x_prompt, x_sample, norm_g, ffn_w_gate, ffn_w_up, ffn_w_down, ssd_w_in, ssd_conv_w, ssd_conv_b,
           ssd_dt_bias, ssd_A_log, ssd_D, ssd_norm_g, ssd_w_out, attn_w_qkv, attn_sink, attn_w_out, rel_bias):
    depth = norm_g.shape[0]
    ffn = [[(ffn_w_gate[i, s].astype(MXU_DTYPE), ffn_w_up[i, s].astype(MXU_DTYPE),
             ffn_w_down[i, s].astype(MXU_DTYPE)) for s in range(2)] for i in range(depth)]
    mixers = []
    for i in range(depth):
        j = i // 2
        if i % 2 == 0:
            mixers.append(_prep_ssd(ssd_w_in[j], ssd_conv_w[j], ssd_conv_b[j], ssd_dt_bias[j], ssd_A_log[j],
                                    ssd_D[j], ssd_norm_g[j], ssd_w_out[j]))
        else:
            mixers.append(_prep_attn(attn_w_qkv[j], attn_sink[j], attn_w_out[j]))
    return tuple(_encoder(x, norm_g, ffn, mixers, rel_bias, TOKEN_TILE) for x in (x_prompt, x_sample))
```

```python
import functools

import jax
import jax.numpy as jnp
import numpy as np
from jax import lax
from jax.experimental import pallas as pl
from jax.experimental.pallas import tpu as pltpu

F32 = jnp.float32
MXU_DTYPE = jnp.bfloat16

EPS = 1e-6
SSD_HEAD_DIM = 64
N_GROUPS = 8
D_STATE = 128
CHUNK = 128
D_CONV = 5
CONV_PAD = D_CONV // 2
HALO = 8
HEAD_DIM = 64
N_KV_HEADS = 4
BLOCK = 128
WINDOW = 128
N_BUCKETS = 32

V7X_VMEM_BYTES = 64 * 1024 * 1024
VMEM_LIMIT_BYTES = V7X_VMEM_BYTES - 8 * 1024 * 1024

NEG_INF = float("-inf")


def _rmsnorm(x, g):
    return x * lax.rsqrt(jnp.mean(x * x, axis=-1, keepdims=True) + EPS) * g


def _silu(x):
    return x * jax.nn.sigmoid(x)


def _mm(a, b):
    return jnp.dot(a.astype(MXU_DTYPE), b.astype(MXU_DTYPE), preferred_element_type=F32)


def _mm_nt(a, b):
    return lax.dot_general(a.astype(MXU_DTYPE), b.astype(MXU_DTYPE), (((1,), (1,)), ((), ())),
                           preferred_element_type=F32)


def _mm_f32(a, b):
    return jnp.dot(a, b, precision=lax.Precision.HIGHEST, preferred_element_type=F32)


def _const_spec(shape):
    zeros = (0,) * len(shape)
    return pl.BlockSpec(shape, lambda *_: zeros, pipeline_mode=pl.Buffered(1))


def _params(n_axes):
    return pltpu.CompilerParams(dimension_semantics=("arbitrary",) * n_axes,
                                vmem_limit_bytes=VMEM_LIMIT_BYTES)


def _ffn_body(x_ref, gpre_ref, gpost_ref, wg_ref, wu_ref, wd_ref, o_ref):
    x = x_ref[...]
    h = _rmsnorm(x, gpre_ref[...]).astype(MXU_DTYPE)
    g = jnp.dot(h, wg_ref[...], preferred_element_type=F32)
    u = jnp.dot(h, wu_ref[...], preferred_element_type=F32)
    a = (_silu(g) * u).astype(MXU_DTYPE)
    y = jnp.dot(a, wd_ref[...], preferred_element_type=F32)
    o_ref[...] = x + 0.5 * _rmsnorm(y, gpost_ref[...])


def _ffn(x, g_pre, g_post, wg, wu, wd, tm):
    m, d = x.shape
    f = wg.shape[1]
    assert m % tm == 0
    row = pl.BlockSpec((tm, d), lambda i: (i, 0))
    return pl.pallas_call(
        _ffn_body,
        grid=(m // tm,),
        in_specs=[row, _const_spec((1, d)), _const_spec((1, d)),
                  _const_spec((d, f)), _const_spec((d, f)), _const_spec((f, d))],
        out_specs=row,
        out_shape=jax.ShapeDtypeStruct((m, d), x.dtype),
        compiler_params=_params(1),
        name="ffn",
    )(x, g_pre, g_post, wg, wu, wd)


def _norm_proj_body(n_w, n_wt, x_ref, g_ref, *refs):
    w_refs = refs[:n_w]
    wt_refs = refs[n_w:n_w + n_wt]
    o_refs = refs[n_w + n_wt:]
    h = _rmsnorm(x_ref[...], g_ref[...]).astype(MXU_DTYPE)
    for w_ref, o_ref in zip(w_refs, o_refs[:n_w]):
        o_ref[...] = jnp.dot(h, w_ref[...], preferred_element_type=F32).astype(o_ref.dtype)
    for wt_ref, o_ref in zip(wt_refs, o_refs[n_w:]):
        o_ref[...] = _mm_nt(wt_ref[...], h).astype(o_ref.dtype)


def _norm_proj(x, g, ws, wts, tm, dtypes=None):
    m, d = x.shape
    assert m % tm == 0
    dtypes = dtypes or [F32] * (len(ws) + len(wts))
    row = lambda n: pl.BlockSpec((tm, n), lambda i: (i, 0))
    col = lambda n: pl.BlockSpec((n, tm), lambda i: (0, i))
    return pl.pallas_call(
        functools.partial(_norm_proj_body, len(ws), len(wts)),
        grid=(m // tm,),
        in_specs=[row(d), _const_spec((1, d))] + [_const_spec(w.shape) for w in ws]
                 + [_const_spec(w.shape) for w in wts],
        out_specs=[row(w.shape[1]) for w in ws] + [col(w.shape[0]) for w in wts],
        out_shape=[jax.ShapeDtypeStruct((m, w.shape[1]), t) for w, t in zip(ws, dtypes)]
                  + [jax.ShapeDtypeStruct((w.shape[0], m), t) for w, t in zip(wts, dtypes[len(ws):])],
        compiler_params=_params(1),
        name="norm_proj",
    )(x, g, *ws, *wts)


def _scan_chunk(xs, bm, cm, dt, dt_t, alog, alog_t, state_ref, y_ref, reverse):
    length = xs.shape[0]
    n_heads = dt.shape[1]
    heads_per_group = n_heads // N_GROUPS
    group_width = heads_per_group * SSD_HEAD_DIM
    assert group_width % 128 == 0 and 128 % SSD_HEAD_DIM == 0
    heads_per_tile = 128 // SSD_HEAD_DIM
    row = lax.broadcasted_iota(jnp.int32, (length, length), 0)
    col = lax.broadcasted_iota(jnp.int32, (length, length), 1)
    valid = (col >= row) if reverse else (col <= row)
    valid_t = (col <= row) if reverse else (col >= row)
    a = dt * (-jnp.exp(alog))
    a_t = dt_t * (-jnp.exp(alog_t))
    acs = _mm_f32(valid.astype(F32), a)
    acs_t = _mm_f32(a_t, valid_t.astype(F32))
    tot_t = _mm_f32(a_t, jnp.ones((length, length), F32))
    end = 0 if reverse else length - 1
    w_t = dt_t * jnp.exp(tot_t - acs_t)

    stack_row = lax.broadcasted_iota(jnp.int32, (heads_per_group * length, group_width), 0)
    stack_col = lax.broadcasted_iota(jnp.int32, (heads_per_group * length, group_width), 1)
    block_diag = (stack_row // length) == (stack_col // SSD_HEAD_DIM)
    head_in_tile = lax.broadcasted_iota(jnp.int32, (length, 128), 1) // SSD_HEAD_DIM

    for g in range(N_GROUPS):
        bg = bm[:, g * D_STATE:(g + 1) * D_STATE]
        cg = cm[:, g * D_STATE:(g + 1) * D_STATE].astype(MXU_DTYPE)
        cb = _mm_nt(cg, bg)
        bg_t = bg.T
        lhs_y, lhs_s, acs_cols = [], [], []
        for r in range(heads_per_group):
            h = g * heads_per_group + r
            acs_col = jnp.broadcast_to(acs[:, h:h + 1], (length, length))
            seg = jnp.exp(jnp.where(valid, acs_col - acs_t[h:h + 1, :], NEG_INF))
            lhs_y.append((cb * seg * dt_t[h:h + 1, :]).astype(MXU_DTYPE))
            lhs_s.append((bg_t * w_t[h:h + 1, :]).astype(MXU_DTYPE))
            acs_cols.append(acs_col)
        lhs = jnp.concatenate([jnp.concatenate(lhs_y, axis=1), jnp.concatenate(lhs_s, axis=1)], axis=0)
        lanes = slice(g * group_width, (g + 1) * group_width)
        x_g = xs[:, lanes].astype(MXU_DTYPE)
        rhs = jnp.where(block_diag, jnp.concatenate([x_g] * heads_per_group, axis=0), jnp.zeros((), MXU_DTYPE))
        out = jnp.dot(lhs, rhs, preferred_element_type=F32)
        tiles = []
        for t in range(group_width // 128):
            tile = acs_cols[t * heads_per_tile]
            for p in range(1, heads_per_tile):
                tile = jnp.where(head_in_tile == p, acs_cols[t * heads_per_tile + p], tile)
            tiles.append(tile)
        off = jnp.exp(jnp.concatenate(tiles, axis=1))
        prev = state_ref[g]
        y_ref[:, lanes] = out[:length] + _mm(cg, prev) * off
        state_ref[g] = prev * off[end:end + 1, :] + out[length:]


def _ssd_bwd_body(n_heads, xbc_ref, prev_ref, next_ref, dt_ref, dtt_ref, sel_ref, convw_ref, convb_ref,
                  bias_ref, biast_ref, alog_ref, alogt_ref,
                  act_ref, dtp_ref, dtpt_ref, y_ref, state_ref):
    i = pl.program_id(1)
    n_chunks = pl.num_programs(1)
    chunk = n_chunks - 1 - i
    length = xbc_ref.shape[0]
    d_inner = n_heads * SSD_HEAD_DIM
    gn = N_GROUPS * D_STATE

    @pl.when(i == 0)
    def _():
        state_ref[...] = jnp.zeros_like(state_ref)

    ext = jnp.concatenate([jnp.where(chunk > 0, prev_ref[...], 0.0), xbc_ref[...],
                           jnp.where(chunk < n_chunks - 1, next_ref[...], 0.0)], axis=0)
    taps = jnp.concatenate([(ext * convw_ref[k:k + 1, :]).astype(MXU_DTYPE) for k in range(D_CONV)], axis=0)
    act = _silu(jnp.dot(sel_ref[...], taps, preferred_element_type=F32) + convb_ref[...])
    act_ref[...] = act

    dtp = jax.nn.softplus(dt_ref[...] + bias_ref[...])
    dtp_t = jax.nn.softplus(dtt_ref[...] + biast_ref[...])
    dtp_ref[...] = dtp
    dtpt_ref[...] = dtp_t

    _scan_chunk(act[:, :d_inner], act[:, d_inner:d_inner + gn], act[:, d_inner + gn:],
                dtp[:, n_heads:], dtp_t[n_heads:, :], alog_ref[:, n_heads:], alogt_ref[n_heads:, :],
                state_ref, y_ref, reverse=True)


def _conv_select():
    ext = CHUNK + 2 * HALO
    t = np.arange(CHUNK)[:, None]
    j = np.arange(D_CONV * ext)[None, :]
    return jnp.asarray((j % ext) == t + HALO - CONV_PAD + j // ext, dtype=MXU_DTYPE)


def _ssd_bwd(xbc, dt_raw, dt_raw_t, conv_w, conv_b, dt_bias, dt_bias_t, alog, alog_t, batch, n_heads):
    m, c = xbc.shape
    seq = m // batch
    assert seq % CHUNK == 0
    nc = seq // CHUNK
    hb = CHUNK // HALO
    d_inner = n_heads * SSD_HEAD_DIM
    sel = _conv_select()
    rev = lambda b, i: b * nc + (nc - 1 - i)
    row = lambda n: pl.BlockSpec((CHUNK, n), lambda b, i: (rev(b, i), 0))
    col = lambda n: pl.BlockSpec((n, CHUNK), lambda b, i: (0, rev(b, i)))
    prev = pl.BlockSpec((HALO, c), lambda b, i: (jnp.maximum(rev(b, i) * hb - 1, 0), 0))
    nxt = pl.BlockSpec((HALO, c), lambda b, i: (jnp.minimum((rev(b, i) + 1) * hb, m // HALO - 1), 0))
    return pl.pallas_call(
        functools.partial(_ssd_bwd_body, n_heads),
        grid=(batch, nc),
        in_specs=[row(c), prev, nxt, row(2 * n_heads), col(2 * n_heads),
                  _const_spec(sel.shape), _const_spec(conv_w.shape), _const_spec(conv_b.shape),
                  _const_spec(dt_bias.shape), _const_spec(dt_bias_t.shape),
                  _const_spec(alog.shape), _const_spec(alog_t.shape)],
        out_specs=[row(c), row(2 * n_heads), col(2 * n_heads), row(d_inner)],
        out_shape=[jax.ShapeDtypeStruct((m, c), F32),
                   jax.ShapeDtypeStruct((m, 2 * n_heads), F32),
                   jax.ShapeDtypeStruct((2 * n_heads, m), F32),
                   jax.ShapeDtypeStruct((m, d_inner), F32)],
        scratch_shapes=[pltpu.VMEM((N_GROUPS, D_STATE, d_inner // N_GROUPS), F32)],
        compiler_params=_params(2),
        name="ssd_bwd",
    )(xbc, xbc, xbc, dt_raw, dt_raw_t, sel, conv_w, conv_b, dt_bias, dt_bias_t, alog, alog_t)


def _ssd_fwd_body(n_heads, act_ref, dtp_ref, dtpt_ref, ybwd_ref, z_ref, x_ref, alog_ref, alogt_ref,
                  dskip_ref, normg_ref, wout_ref, gpost_ref, o_ref, y_ref, state_ref):
    d_inner = n_heads * SSD_HEAD_DIM
    gn = N_GROUPS * D_STATE
    group_width = d_inner // N_GROUPS

    @pl.when(pl.program_id(1) == 0)
    def _():
        state_ref[...] = jnp.zeros_like(state_ref)

    xs = act_ref[:, :d_inner]
    _scan_chunk(xs, act_ref[:, d_inner:d_inner + gn], act_ref[:, d_inner + gn:],
                dtp_ref[:, :n_heads], dtpt_ref[:n_heads, :], alog_ref[:, :n_heads], alogt_ref[:n_heads, :],
                state_ref, y_ref, reverse=False)

    y = (y_ref[...] + ybwd_ref[...] + xs * dskip_ref[...]) * _silu(z_ref[...])
    for g in range(N_GROUPS):
        lanes = slice(g * group_width, (g + 1) * group_width)
        yg = y[:, lanes]
        yg = yg * lax.rsqrt(jnp.mean(yg * yg, axis=-1, keepdims=True) + EPS)
        y_ref[:, lanes] = yg * normg_ref[:, lanes]
    mix = jnp.dot(y_ref[...].astype(MXU_DTYPE), wout_ref[...], preferred_element_type=F32)
    o_ref[...] = x_ref[...] + _rmsnorm(mix, gpost_ref[...])


def _ssd_fwd(act, dtp, dtp_t, y_bwd, z, x, alog, alog_t, d_skip, norm_g, w_out, g_post, batch, n_heads):
    m, c = act.shape
    d = x.shape[1]
    seq = m // batch
    nc = seq // CHUNK
    d_inner = n_heads * SSD_HEAD_DIM
    row = lambda n: pl.BlockSpec((CHUNK, n), lambda b, i: (b * nc + i, 0))
    col = lambda n: pl.BlockSpec((n, CHUNK), lambda b, i: (0, b * nc + i))
    return pl.pallas_call(
        functools.partial(_ssd_fwd_body, n_heads),
        grid=(batch, nc),
        in_specs=[row(c), row(2 * n_heads), col(2 * n_heads), row(d_inner), row(d_inner), row(d),
                  _const_spec(alog.shape), _const_spec(alog_t.shape), _const_spec(d_skip.shape),
                  _const_spec(norm_g.shape), _const_spec(w_out.shape), _const_spec(g_post.shape)],
        out_specs=row(d),
        out_shape=jax.ShapeDtypeStruct((m, d), F32),
        scratch_shapes=[pltpu.VMEM((CHUNK, d_inner), F32),
                        pltpu.VMEM((N_GROUPS, D_STATE, d_inner // N_GROUPS), F32)],
        compiler_params=_params(2),
        name="ssd_fwd",
    )(act, dtp, dtp_t, y_bwd, z, x, alog, alog_t, d_skip, norm_g, w_out, g_post)


def _t5_bias_table(relb_ref, bias_ref, group):
    kj = lax.broadcasted_iota(jnp.int32, (3 * BLOCK, BLOCK), 0)
    qi = lax.broadcasted_iota(jnp.int32, (3 * BLOCK, BLOCK), 1)
    rel = kj - BLOCK - qi
    n = jnp.abs(rel)
    half = N_BUCKETS // 2
    max_exact = half // 2
    large = jnp.full_like(n, max_exact)
    for k in range(1, half - max_exact):
        large = large + (n * n >= (max_exact * max_exact) << k).astype(jnp.int32)
    bucket = jnp.where(rel > 0, half, 0) + jnp.where(n < max_exact, n, large)
    in_window = n <= WINDOW
    for kv in range(bias_ref.shape[0]):
        for r in range(group):
            acc = jnp.zeros((3 * BLOCK, BLOCK), F32)
            for b in range(N_BUCKETS):
                acc = jnp.where(bucket == b, relb_ref[b, kv * group + r], acc)
            bias_ref[kv, :, r * BLOCK:(r + 1) * BLOCK] = jnp.where(in_window, acc, NEG_INF)


def _attn_body(qt_ref, kp_ref, kc_ref, kn_ref, vtp_ref, vtc_ref, vtn_ref, x_ref, sink_ref, relb_ref,
               wout_ref, gpost_ref, o_ref, bias_ref, ctx_ref):
    blk = pl.program_id(1)
    n_blocks = pl.num_programs(1)
    n_kv = bias_ref.shape[0]
    group = qt_ref.shape[0] // HEAD_DIM // n_kv
    width = group * BLOCK

    @pl.when((pl.program_id(0) == 0) & (blk == 0))
    def _():
        _t5_bias_table(relb_ref, bias_ref, group)

    kj = lax.broadcasted_iota(jnp.int32, (3 * BLOCK, width), 0)
    in_seq = ((kj >= BLOCK) | (blk > 0)) & ((kj < 2 * BLOCK) | (blk < n_blocks - 1))

    kb = jnp.concatenate([kp_ref[...], kc_ref[...], kn_ref[...]], axis=0)
    vtb = jnp.concatenate([vtp_ref[...], vtc_ref[...], vtn_ref[...]], axis=1)
    heads_per_lane_tile = 128 // HEAD_DIM
    for kv in range(n_kv):
        tile = kv // heads_per_lane_tile
        k_tile = kb[:, tile * 128:(tile + 1) * 128]
        q_g = jnp.concatenate([qt_ref[(kv * group + r) * HEAD_DIM:(kv * group + r + 1) * HEAD_DIM, :]
                               for r in range(group)], axis=1)
        q_pad = jnp.concatenate([q_g if p == kv % heads_per_lane_tile else jnp.zeros_like(q_g)
                                 for p in range(heads_per_lane_tile)], axis=0)
        scores = jnp.dot(k_tile, q_pad, preferred_element_type=F32)
        scores = jnp.where(in_seq, scores + bias_ref[kv], NEG_INF)
        sink = sink_ref[:, kv * width:(kv + 1) * width]
        mx = jnp.maximum(jnp.max(scores, axis=0, keepdims=True), sink)
        e = jnp.exp(scores - mx)
        denom = jnp.sum(e, axis=0, keepdims=True) + jnp.exp(sink - mx)
        vt_h = vtb[kv * HEAD_DIM:(kv + 1) * HEAD_DIM, :]
        ctx_t = jnp.dot(vt_h, e.astype(MXU_DTYPE), preferred_element_type=F32) / denom
        for r in range(group):
            h = kv * group + r
            ctx_ref[h * HEAD_DIM:(h + 1) * HEAD_DIM, :] = ctx_t[:, r * BLOCK:(r + 1) * BLOCK]
    ctx = ctx_ref[...].T
    mix = jnp.dot(ctx.astype(MXU_DTYPE), wout_ref[...], preferred_element_type=F32)
    o_ref[...] = x_ref[...] + _rmsnorm(mix, gpost_ref[...])


def _attn(q_t, k, v_t, x, sink_row, rel_bias, w_out, g_post, batch):
    qd, m = q_t.shape
    kvd = k.shape[1]
    d = x.shape[1]
    n_kv = kvd // HEAD_DIM
    group = qd // kvd
    assert (m // batch) % BLOCK == 0
    nb = m // batch // BLOCK
    cur_i = lambda b, i: b * nb + i
    prev_i = lambda b, i: b * nb + jnp.maximum(i - 1, 0)
    next_i = lambda b, i: b * nb + jnp.minimum(i + 1, nb - 1)
    rows = lambda n, f: pl.BlockSpec((BLOCK, n), lambda b, i: (f(b, i), 0))
    cols = lambda n, f: pl.BlockSpec((n, BLOCK), lambda b, i: (0, f(b, i)))
    smem = pl.BlockSpec(memory_space=pltpu.SMEM)
    return pl.pallas_call(
        _attn_body,
        grid=(batch, nb),
        in_specs=[cols(qd, cur_i), rows(kvd, prev_i), rows(kvd, cur_i), rows(kvd, next_i),
                  cols(kvd, prev_i), cols(kvd, cur_i), cols(kvd, next_i), rows(d, cur_i),
                  _const_spec(sink_row.shape), smem, _const_spec(w_out.shape), _const_spec(g_post.shape)],
        out_specs=rows(d, cur_i),
        out_shape=jax.ShapeDtypeStruct((m, d), F32),
        scratch_shapes=[pltpu.VMEM((n_kv, 3 * BLOCK, group * BLOCK), F32),
                        pltpu.VMEM((qd, BLOCK), F32)],
        compiler_params=_params(2),
        name="attn",
    )(q_t, k, k, k, v_t, v_t, v_t, x, sink_row, rel_bias, w_out, g_post)


def _row(v):
    return v.reshape(1, -1)


def _lane_bcast(r):
    return jnp.broadcast_to(r.reshape(-1, 1), (r.shape[1], CHUNK))


def _prep_ssd(w_in, conv_w, conv_b, dt_bias, a_log, d_skip, norm_g, w_out):
    n_heads = d_skip.shape[0]
    d_inner = n_heads * SSD_HEAD_DIM
    conv_dim = conv_w.shape[1]
    w_dt = w_in[:, d_inner + conv_dim:].astype(MXU_DTYPE)
    bias = dt_bias.reshape(1, -1)
    alog = a_log.reshape(1, -1)
    return dict(n_heads=n_heads, w_z=w_in[:, :d_inner].astype(MXU_DTYPE),
                w_xbc=w_in[:, d_inner:d_inner + conv_dim].astype(MXU_DTYPE), w_dt=w_dt, w_dt_t=w_dt.T,
                conv_w=conv_w, conv_b=_row(conv_b), bias=bias, bias_t=_lane_bcast(bias),
                alog=alog, alog_t=_lane_bcast(alog), d_skip=_row(jnp.repeat(d_skip, SSD_HEAD_DIM)),
                norm_g=_row(norm_g), w_out=w_out.astype(MXU_DTYPE))


def _ssd_layer(x, batch, g_pre, g_post, p, tm):
    z, xbc, dt_raw, dt_raw_t = _norm_proj(x, g_pre, [p["w_z"], p["w_xbc"], p["w_dt"]], [p["w_dt_t"]], tm)
    act, dtp, dtp_t, y_bwd = _ssd_bwd(xbc, dt_raw, dt_raw_t, p["conv_w"], p["conv_b"], p["bias"], p["bias_t"],
                                      p["alog"], p["alog_t"], batch, p["n_heads"])
    return _ssd_fwd(act, dtp, dtp_t, y_bwd, z, x, p["alog"], p["alog_t"], p["d_skip"], p["norm_g"],
                    p["w_out"], g_post, batch, p["n_heads"])


def _prep_attn(w_qkv, sink, w_out):
    q_dim = w_out.shape[0]
    kv_dim = (w_qkv.shape[1] - q_dim) // 2
    w_q_t = (w_qkv[:, :q_dim] * (HEAD_DIM ** -0.5)).T.astype(MXU_DTYPE)
    return dict(w_q_t=w_q_t, w_k=w_qkv[:, q_dim:q_dim + kv_dim].astype(MXU_DTYPE),
                w_v_t=w_qkv[:, q_dim + kv_dim:].T.astype(MXU_DTYPE),
                sink_row=_row(jnp.repeat(sink, BLOCK)), w_out=w_out.astype(MXU_DTYPE))


def _attn_layer(x, batch, g_pre, g_post, p, rel_bias, tm):
    k, q_t, v_t = _norm_proj(x, g_pre, [p["w_k"]], [p["w_q_t"], p["w_v_t"]], tm, dtypes=[MXU_DTYPE] * 3)
    return _attn(q_t, k, v_t, x, p["sink_row"], rel_bias, p["w_out"], g_post, batch)


def _encoder(x3, norm_g, ffn, mixers, rel_bias, tm):
    batch, seq, d = x3.shape
    x = x3.reshape(batch * seq, d)
    for i, mixer in enumerate(mixers):
        ng = [_row(norm_g[i, s]) for s in range(norm_g.shape[1])]
        x = _ffn(x, ng[0], ng[1], *ffn[i][0], tm)
        if "w_z" in mixer:
            x = _ssd_layer(x, batch, ng[2], ng[3], mixer, tm)
        else:
            x = _attn_layer(x, batch, ng[2], ng[3], mixer, rel_bias, tm)
        x = _ffn(x, ng[4], ng[5], *ffn[i][1], tm)
    return x.reshape(batch, seq, d)


TOKEN_TILE = 512


def kernel(x_prompt, x_sample, norm_g, ffn_w_gate, ffn_w_up, ffn_w_down, ssd_w_in, ssd_conv_w, ssd_conv_b,
           ssd_dt_bias, ssd_A_log, ssd_D, ssd_norm_g, ssd_w_out, attn_w_qkv, attn_sink, attn_w_out, rel_bias):
    depth = norm_g.shape[0]
    ffn = [[(ffn_w_gate[i, s].astype(MXU_DTYPE), ffn_w_up[i, s].astype(MXU_DTYPE),
             ffn_w_down[i, s].astype(MXU_DTYPE)) for s in range(2)] for i in range(depth)]
    mixers = []
    for i in range(depth):
        j = i // 2
        if i % 2 == 0:
            mixers.append(_prep_ssd(ssd_w_in[j], ssd_conv_w[j], ssd_conv_b[j], ssd_dt_bias[j], ssd_A_log[j],
                                    ssd_D[j], ssd_norm_g[j], ssd_w_out[j]))
        else:
            mixers.append(_prep_attn(attn_w_qkv[j], attn_sink[j], attn_w_out[j]))
    return tuple(_encoder(x, norm_g, ffn, mixers, rel_bias, TOKEN_TILE) for x in (x_prompt, x_sample))
```

```python
import functools

import jax
import jax.numpy as jnp
import numpy as np
from jax import lax
from jax.experimental import pallas as pl
from jax.experimental.pallas import tpu as pltpu

F32 = jnp.float32
MXU_DTYPE = jnp.bfloat16

EPS = 1e-6
SSD_HEAD_DIM = 64
N_GROUPS = 8
D_STATE = 128
CHUNK = 128
D_CONV = 5
CONV_PAD = D_CONV // 2
HALO = 8
SSD_STEP = 4 * CHUNK
HEAD_DIM = 64
N_KV_HEADS = 4
BLOCK = 128
WINDOW = 128
N_BUCKETS = 32
FFN_TILE = 1024
FFN_SUBTILE = 256
PROJ_TILE = 512
ATTN_STEP = 2 * BLOCK
N_EDGE_VARIANTS = 4

V7X_VMEM_BYTES = 64 * 1024 * 1024
VMEM_LIMIT_BYTES = V7X_VMEM_BYTES - 8 * 1024 * 1024

NEG_INF = float("-inf")
LOG2_E = 1.4426950408889634


def _rmsnorm(x, g):
    return x * lax.rsqrt(jnp.mean(x * x, axis=-1, keepdims=True) + EPS) * g


def _silu(x):
    t = 0.5 * x
    return t + t * jnp.tanh(t)


def _mm(a, b):
    return jnp.dot(a.astype(MXU_DTYPE), b.astype(MXU_DTYPE), preferred_element_type=F32)


def _mm_nt(a, b):
    return lax.dot_general(a.astype(MXU_DTYPE), b.astype(MXU_DTYPE), (((1,), (1,)), ((), ())),
                           preferred_element_type=F32)


def _mm_f32(a, b):
    return jnp.dot(a, b, precision=lax.Precision.HIGHEST, preferred_element_type=F32)


def _const_spec(shape):
    zeros = (0,) * len(shape)
    return pl.BlockSpec(shape, lambda *_: zeros, pipeline_mode=pl.Buffered(1))


def _params(n_axes):
    return pltpu.CompilerParams(dimension_semantics=("arbitrary",) * n_axes,
                                vmem_limit_bytes=VMEM_LIMIT_BYTES)


def _ffn_body(x_ref, gpre_ref, gpost_ref, wg_ref, wu_ref, wd_ref, o_ref):
    for s in range(x_ref.shape[0] // FFN_SUBTILE):
        rows = slice(s * FFN_SUBTILE, (s + 1) * FFN_SUBTILE)
        x = x_ref[rows, :]
        h = _rmsnorm(x, gpre_ref[...]).astype(MXU_DTYPE)
        g = jnp.dot(h, wg_ref[...], preferred_element_type=F32)
        u = jnp.dot(h, wu_ref[...], preferred_element_type=F32)
        a = (_silu(g) * u).astype(MXU_DTYPE)
        y = jnp.dot(a, wd_ref[...], preferred_element_type=F32)
        o_ref[rows, :] = x + 0.5 * _rmsnorm(y, gpost_ref[...])


def _ffn(x, g_pre, g_post, wg, wu, wd):
    m, d = x.shape
    f = wg.shape[1]
    tm = FFN_TILE
    assert m % tm == 0 and tm % FFN_SUBTILE == 0
    row = pl.BlockSpec((tm, d), lambda i: (i, 0))
    return pl.pallas_call(
        _ffn_body,
        grid=(m // tm,),
        in_specs=[row, _const_spec((1, d)), _const_spec((1, d)),
                  _const_spec((d, f)), _const_spec((d, f)), _const_spec((f, d))],
        out_specs=row,
        out_shape=jax.ShapeDtypeStruct((m, d), x.dtype),
        compiler_params=_params(1),
        name="ffn",
    )(x, g_pre, g_post, wg, wu, wd)


def _norm_proj_body(n_w, n_wt, x_ref, g_ref, *refs):
    w_refs = refs[:n_w]
    wt_refs = refs[n_w:n_w + n_wt]
    o_refs = refs[n_w + n_wt:]
    h = _rmsnorm(x_ref[...], g_ref[...]).astype(MXU_DTYPE)
    for w_ref, o_ref in zip(w_refs, o_refs[:n_w]):
        o_ref[...] = jnp.dot(h, w_ref[...], preferred_element_type=F32).astype(o_ref.dtype)
    for wt_ref, o_ref in zip(wt_refs, o_refs[n_w:]):
        o_ref[...] = _mm_nt(wt_ref[...], h).astype(o_ref.dtype)


def _norm_proj(x, g, ws, wts, dtypes=None):
    m, d = x.shape
    tm = PROJ_TILE
    assert m % tm == 0
    dtypes = dtypes or [F32] * (len(ws) + len(wts))
    row = lambda n: pl.BlockSpec((tm, n), lambda i: (i, 0))
    col = lambda n: pl.BlockSpec((n, tm), lambda i: (0, i))
    return pl.pallas_call(
        functools.partial(_norm_proj_body, len(ws), len(wts)),
        grid=(m // tm,),
        in_specs=[row(d), _const_spec((1, d))] + [_const_spec(w.shape) for w in ws]
                 + [_const_spec(w.shape) for w in wts],
        out_specs=[row(w.shape[1]) for w in ws] + [col(w.shape[0]) for w in wts],
        out_shape=[jax.ShapeDtypeStruct((m, w.shape[1]), t) for w, t in zip(ws, dtypes)]
                  + [jax.ShapeDtypeStruct((w.shape[0], m), t) for w, t in zip(wts, dtypes[len(ws):])],
        compiler_params=_params(1),
        name="norm_proj",
    )(x, g, *ws, *wts)


def _scan_chunk(xs, bm, cm, dt, dt_t, alog, alog_t, state_ref, y_ref, reverse):
    length = xs.shape[0]
    n_heads = dt.shape[1]
    heads_per_group = n_heads // N_GROUPS
    group_width = heads_per_group * SSD_HEAD_DIM
    assert group_width % 128 == 0 and 128 % SSD_HEAD_DIM == 0
    heads_per_tile = 128 // SSD_HEAD_DIM
    row = lax.broadcasted_iota(jnp.int32, (length, length), 0)
    col = lax.broadcasted_iota(jnp.int32, (length, length), 1)
    valid = (col >= row) if reverse else (col <= row)
    valid_t = (col <= row) if reverse else (col >= row)
    a = dt * (-LOG2_E * jnp.exp(alog))
    a_t = dt_t * (-LOG2_E * jnp.exp(alog_t))
    acs = _mm_f32(valid.astype(F32), a)
    acs_t = _mm_f32(a_t, valid_t.astype(F32))
    tot_t = _mm_f32(a_t, jnp.ones((length, length), F32))
    end = 0 if reverse else length - 1
    w_t = dt_t * jnp.exp2(tot_t - acs_t)
    src_t = acs_t - jnp.log2(dt_t)

    stack_row = lax.broadcasted_iota(jnp.int32, (heads_per_group * length, group_width), 0)
    stack_col = lax.broadcasted_iota(jnp.int32, (heads_per_group * length, group_width), 1)
    block_diag = (stack_row // length) == (stack_col // SSD_HEAD_DIM)
    head_in_tile = lax.broadcasted_iota(jnp.int32, (length, 128), 1) // SSD_HEAD_DIM

    for g in range(N_GROUPS):
        bg = bm[:, g * D_STATE:(g + 1) * D_STATE]
        cg = cm[:, g * D_STATE:(g + 1) * D_STATE].astype(MXU_DTYPE)
        cb = _mm_nt(cg, bg)
        bg_t = bg.T
        lhs_y, lhs_s, acs_cols = [], [], []
        for r in range(heads_per_group):
            h = g * heads_per_group + r
            acs_col = jnp.broadcast_to(acs[:, h:h + 1], (length, length))
            seg_dt = jnp.exp2(jnp.where(valid, acs_col - src_t[h:h + 1, :], NEG_INF))
            lhs_y.append((cb * seg_dt).astype(MXU_DTYPE))
            lhs_s.append((bg_t * w_t[h:h + 1, :]).astype(MXU_DTYPE))
            acs_cols.append(acs_col)
        lhs = jnp.concatenate([jnp.concatenate(lhs_y, axis=1), jnp.concatenate(lhs_s, axis=1)], axis=0)
        lanes = slice(g * group_width, (g + 1) * group_width)
        x_g = xs[:, lanes].astype(MXU_DTYPE)
        rhs = jnp.where(block_diag, jnp.concatenate([x_g] * heads_per_group, axis=0), jnp.zeros((), MXU_DTYPE))
        out = jnp.dot(lhs, rhs, preferred_element_type=F32)
        tiles = []
        for t in range(group_width // 128):
            tile = acs_cols[t * heads_per_tile]
            for p in range(1, heads_per_tile):
                tile = jnp.where(head_in_tile == p, acs_cols[t * heads_per_tile + p], tile)
            tiles.append(tile)
        off = jnp.exp2(jnp.concatenate(tiles, axis=1))
        prev = state_ref[g]
        y_ref[:, lanes] = out[:length] + _mm(cg, prev) * off
        state_ref[g] = prev * off[end:end + 1, :] + out[length:]


def _ssd_bwd_body(n_heads, xbc_ref, prev_ref, next_ref, dt_ref, dtt_ref, sel_ref, convw_ref, convb_ref,
                  bias_ref, biast_ref, alog_ref, alogt_ref,
                  act_ref, dtp_ref, dtpt_ref, y_ref, state_ref):
    i = pl.program_id(1)
    n_steps = pl.num_programs(1)
    step = n_steps - 1 - i
    d_inner = n_heads * SSD_HEAD_DIM
    gn = N_GROUPS * D_STATE

    @pl.when(i == 0)
    def _():
        state_ref[...] = jnp.zeros_like(state_ref)

    dtp = jax.nn.softplus(dt_ref[...] + bias_ref[...])
    dtp_t = jax.nn.softplus(dtt_ref[...] + biast_ref[...])
    dtp_ref[...] = dtp
    dtpt_ref[...] = dtp_t

    ext_all = jnp.concatenate([jnp.where(step > 0, prev_ref[...], 0.0), xbc_ref[...],
                               jnp.where(step < n_steps - 1, next_ref[...], 0.0)], axis=0)
    ext_all = ext_all.astype(MXU_DTYPE)
    conv_w = convw_ref[...].astype(MXU_DTYPE)
    for s in reversed(range(xbc_ref.shape[0] // CHUNK)):
        rows = slice(s * CHUNK, (s + 1) * CHUNK)
        ext = ext_all[s * CHUNK:(s + 1) * CHUNK + 2 * HALO]
        taps = jnp.concatenate([ext * conv_w[k:k + 1, :] for k in range(D_CONV)], axis=0)
        act = _silu(jnp.dot(sel_ref[...], taps, preferred_element_type=F32) + convb_ref[...])
        act_ref[rows, :] = act
        _scan_chunk(act[:, :d_inner], act[:, d_inner:d_inner + gn], act[:, d_inner + gn:],
                    dtp[rows, n_heads:], dtp_t[n_heads:, rows], alog_ref[:, n_heads:], alogt_ref[n_heads:, :],
                    state_ref, y_ref.at[rows], reverse=True)


def _conv_select():
    ext = CHUNK + 2 * HALO
    t = np.arange(CHUNK)[:, None]
    j = np.arange(D_CONV * ext)[None, :]
    return jnp.asarray((j % ext) == t + HALO - CONV_PAD + j // ext, dtype=MXU_DTYPE)


def _ssd_bwd(xbc, dt_raw, dt_raw_t, conv_w, conv_b, dt_bias, dt_bias_t, alog, alog_t, batch, n_heads):
    m, c = xbc.shape
    seq = m // batch
    assert seq % SSD_STEP == 0
    nc = seq // SSD_STEP
    hb = SSD_STEP // HALO
    d_inner = n_heads * SSD_HEAD_DIM
    sel = _conv_select()
    rev = lambda b, i: b * nc + (nc - 1 - i)
    row = lambda n: pl.BlockSpec((SSD_STEP, n), lambda b, i: (rev(b, i), 0))
    col = lambda n: pl.BlockSpec((n, SSD_STEP), lambda b, i: (0, rev(b, i)))
    prev = pl.BlockSpec((HALO, c), lambda b, i: (jnp.maximum(rev(b, i) * hb - 1, 0), 0))
    nxt = pl.BlockSpec((HALO, c), lambda b, i: (jnp.minimum((rev(b, i) + 1) * hb, m // HALO - 1), 0))
    return pl.pallas_call(
        functools.partial(_ssd_bwd_body, n_heads),
        grid=(batch, nc),
        in_specs=[row(c), prev, nxt, row(2 * n_heads), col(2 * n_heads),
                  _const_spec(sel.shape), _const_spec(conv_w.shape), _const_spec(conv_b.shape),
                  _const_spec(dt_bias.shape), _const_spec(dt_bias_t.shape),
                  _const_spec(alog.shape), _const_spec(alog_t.shape)],
        out_specs=[row(c), row(2 * n_heads), col(2 * n_heads), row(d_inner)],
        out_shape=[jax.ShapeDtypeStruct((m, c), F32),
                   jax.ShapeDtypeStruct((m, 2 * n_heads), F32),
                   jax.ShapeDtypeStruct((2 * n_heads, m), F32),
                   jax.ShapeDtypeStruct((m, d_inner), F32)],
        scratch_shapes=[pltpu.VMEM((N_GROUPS, D_STATE, d_inner // N_GROUPS), F32)],
        compiler_params=_params(2),
        name="ssd_bwd",
    )(xbc, xbc, xbc, dt_raw, dt_raw_t, sel, conv_w, conv_b, dt_bias, dt_bias_t, alog, alog_t)


def _ssd_fwd_body(n_heads, act_ref, dtp_ref, dtpt_ref, ybwd_ref, z_ref, x_ref, alog_ref, alogt_ref,
                  dskip_ref, normg_ref, wout_ref, gpost_ref, o_ref, y_ref, state_ref):
    d_inner = n_heads * SSD_HEAD_DIM
    gn = N_GROUPS * D_STATE
    group_width = d_inner // N_GROUPS

    @pl.when(pl.program_id(1) == 0)
    def _():
        state_ref[...] = jnp.zeros_like(state_ref)

    for s in range(act_ref.shape[0] // CHUNK):
        rows = slice(s * CHUNK, (s + 1) * CHUNK)
        _scan_chunk(act_ref[rows, :d_inner], act_ref[rows, d_inner:d_inner + gn], act_ref[rows, d_inner + gn:],
                    dtp_ref[rows, :n_heads], dtpt_ref[:n_heads, rows], alog_ref[:, :n_heads],
                    alogt_ref[:n_heads, :], state_ref, y_ref.at[rows], reverse=False)

    y = (y_ref[...] + ybwd_ref[...] + act_ref[:, :d_inner] * dskip_ref[...]) * _silu(z_ref[...])
    for g in range(N_GROUPS):
        lanes = slice(g * group_width, (g + 1) * group_width)
        yg = y[:, lanes]
        yg = yg * lax.rsqrt(jnp.mean(yg * yg, axis=-1, keepdims=True) + EPS)
        y_ref[:, lanes] = yg * normg_ref[:, lanes]
    mix = jnp.dot(y_ref[...].astype(MXU_DTYPE), wout_ref[...], preferred_element_type=F32)
    o_ref[...] = x_ref[...] + _rmsnorm(mix, gpost_ref[...])


def _ssd_fwd(act, dtp, dtp_t, y_bwd, z, x, alog, alog_t, d_skip, norm_g, w_out, g_post, batch, n_heads):
    m, c = act.shape
    d = x.shape[1]
    seq = m // batch
    assert seq % SSD_STEP == 0
    nc = seq // SSD_STEP
    d_inner = n_heads * SSD_HEAD_DIM
    row = lambda n: pl.BlockSpec((SSD_STEP, n), lambda b, i: (b * nc + i, 0))
    col = lambda n: pl.BlockSpec((n, SSD_STEP), lambda b, i: (0, b * nc + i))
    return pl.pallas_call(
        functools.partial(_ssd_fwd_body, n_heads),
        grid=(batch, nc),
        in_specs=[row(c), row(2 * n_heads), col(2 * n_heads), row(d_inner), row(d_inner), row(d),
                  _const_spec(alog.shape), _const_spec(alog_t.shape), _const_spec(d_skip.shape),
                  _const_spec(norm_g.shape), _const_spec(w_out.shape), _const_spec(g_post.shape)],
        out_specs=row(d),
        out_shape=jax.ShapeDtypeStruct((m, d), F32),
        scratch_shapes=[pltpu.VMEM((SSD_STEP, d_inner), F32),
                        pltpu.VMEM((N_GROUPS, D_STATE, d_inner // N_GROUPS), F32)],
        compiler_params=_params(2),
        name="ssd_fwd",
    )(act, dtp, dtp_t, y_bwd, z, x, alog, alog_t, d_skip, norm_g, w_out, g_post)


def _t5_bias_table(relb_ref, bias_ref, group):
    kj = lax.broadcasted_iota(jnp.int32, (3 * BLOCK, BLOCK), 0)
    qi = lax.broadcasted_iota(jnp.int32, (3 * BLOCK, BLOCK), 1)
    rel = kj - BLOCK - qi
    n = jnp.abs(rel)
    half = N_BUCKETS // 2
    max_exact = half // 2
    large = jnp.full_like(n, max_exact)
    for k in range(1, half - max_exact):
        large = large + (n * n >= (max_exact * max_exact) << k).astype(jnp.int32)
    bucket = jnp.where(rel > 0, half, 0) + jnp.where(n < max_exact, n, large)
    in_window = n <= WINDOW
    n_kv = bias_ref.shape[0] // N_EDGE_VARIANTS
    for kv in range(n_kv):
        for r in range(group):
            acc = jnp.zeros((3 * BLOCK, BLOCK), F32)
            for b in range(N_BUCKETS):
                acc = jnp.where(bucket == b, relb_ref[b, kv * group + r] * LOG2_E, acc)
            acc = jnp.where(in_window, acc, NEG_INF)
            for variant in range(N_EDGE_VARIANTS):
                tbl = acc
                if variant & 1:
                    tbl = jnp.where(kj >= BLOCK, tbl, NEG_INF)
                if variant & 2:
                    tbl = jnp.where(kj < 2 * BLOCK, tbl, NEG_INF)
                bias_ref[variant * n_kv + kv, :, r * BLOCK:(r + 1) * BLOCK] = tbl


def _attn_body(qt_ref, kp_ref, kc_ref, kn_ref, vtp_ref, vtc_ref, vtn_ref, x_ref, sink_ref, relb_ref,
               wout_ref, gpost_ref, o_ref, bias_ref, ctx_ref):
    step = pl.program_id(1)
    blocks_per_step = qt_ref.shape[1] // BLOCK
    last_block = pl.num_programs(1) * blocks_per_step - 1
    n_kv = bias_ref.shape[0] // N_EDGE_VARIANTS
    group = qt_ref.shape[0] // HEAD_DIM // n_kv
    width = group * BLOCK

    @pl.when((pl.program_id(0) == 0) & (step == 0))
    def _():
        _t5_bias_table(relb_ref, bias_ref, group)

    kb_all = jnp.concatenate([kp_ref[...], kc_ref[...], kn_ref[...]], axis=0)
    vtb_all = jnp.concatenate([vtp_ref[...], vtc_ref[...], vtn_ref[...]], axis=1)
    heads_per_lane_tile = 128 // HEAD_DIM
    for sub in range(blocks_per_step):
        blk = step * blocks_per_step + sub
        variant = (blk == 0).astype(jnp.int32) + 2 * (blk == last_block).astype(jnp.int32)
        kb = kb_all[sub * BLOCK:(sub + 3) * BLOCK]
        vtb = vtb_all[:, sub * BLOCK:(sub + 3) * BLOCK]
        q_lanes = slice(sub * BLOCK, (sub + 1) * BLOCK)
        for kv in range(n_kv):
            tile = kv // heads_per_lane_tile
            k_tile = kb[:, tile * 128:(tile + 1) * 128]
            q_g = jnp.concatenate([qt_ref[(kv * group + r) * HEAD_DIM:(kv * group + r + 1) * HEAD_DIM, q_lanes]
                                   for r in range(group)], axis=1)
            q_pad = jnp.concatenate([q_g if p == kv % heads_per_lane_tile else jnp.zeros_like(q_g)
                                     for p in range(heads_per_lane_tile)], axis=0)
            scores = jnp.dot(k_tile, q_pad, preferred_element_type=F32)
            scores = scores + bias_ref[variant * n_kv + kv]
            sink = sink_ref[:, kv * width:(kv + 1) * width]
            mx = jnp.maximum(jnp.max(scores, axis=0, keepdims=True), sink)
            e = jnp.exp2(scores - mx)
            denom = jnp.sum(e, axis=0, keepdims=True) + jnp.exp2(sink - mx)
            vt_h = vtb[kv * HEAD_DIM:(kv + 1) * HEAD_DIM, :]
            ctx_t = jnp.dot(vt_h, e.astype(MXU_DTYPE), preferred_element_type=F32) / denom
            for r in range(group):
                h = kv * group + r
                ctx_ref[h * HEAD_DIM:(h + 1) * HEAD_DIM, q_lanes] = ctx_t[:, r * BLOCK:(r + 1) * BLOCK]
    ctx = ctx_ref[...].T
    mix = jnp.dot(ctx.astype(MXU_DTYPE), wout_ref[...], preferred_element_type=F32)
    o_ref[...] = x_ref[...] + _rmsnorm(mix, gpost_ref[...])


def _attn(q_t, k, v_t, x, sink_row, rel_bias, w_out, g_post, batch):
    qd, m = q_t.shape
    kvd = k.shape[1]
    d = x.shape[1]
    n_kv = kvd // HEAD_DIM
    group = qd // kvd
    assert (m // batch) % ATTN_STEP == 0
    nb = m // batch // BLOCK
    bps = ATTN_STEP // BLOCK
    ns = nb // bps
    cur_i = lambda b, i: b * ns + i
    prev_i = lambda b, i: b * nb + jnp.maximum(i * bps - 1, 0)
    next_i = lambda b, i: b * nb + jnp.minimum((i + 1) * bps, nb - 1)
    rows = lambda size, n, f: pl.BlockSpec((size, n), lambda b, i: (f(b, i), 0))
    cols = lambda size, n, f: pl.BlockSpec((n, size), lambda b, i: (0, f(b, i)))
    smem = pl.BlockSpec(memory_space=pltpu.SMEM)
    return pl.pallas_call(
        _attn_body,
        grid=(batch, ns),
        in_specs=[cols(ATTN_STEP, qd, cur_i),
                  rows(BLOCK, kvd, prev_i), rows(ATTN_STEP, kvd, cur_i), rows(BLOCK, kvd, next_i),
                  cols(BLOCK, kvd, prev_i), cols(ATTN_STEP, kvd, cur_i), cols(BLOCK, kvd, next_i),
                  rows(ATTN_STEP, d, cur_i),
                  _const_spec(sink_row.shape), smem, _const_spec(w_out.shape), _const_spec(g_post.shape)],
        out_specs=rows(ATTN_STEP, d, cur_i),
        out_shape=jax.ShapeDtypeStruct((m, d), F32),
        scratch_shapes=[pltpu.VMEM((N_EDGE_VARIANTS * n_kv, 3 * BLOCK, group * BLOCK), F32),
                        pltpu.VMEM((qd, ATTN_STEP), F32)],
        compiler_params=_params(2),
        name="attn",
    )(q_t, k, k, k, v_t, v_t, v_t, x, sink_row, rel_bias, w_out, g_post)


def _row(v):
    return v.reshape(1, -1)


def _lane_bcast(r, width):
    return jnp.broadcast_to(r.reshape(-1, 1), (r.shape[1], width))


def _prep_ssd(w_in, conv_w, conv_b, dt_bias, a_log, d_skip, norm_g, w_out):
    n_heads = d_skip.shape[0]
    d_inner = n_heads * SSD_HEAD_DIM
    conv_dim = conv_w.shape[1]
    w_dt = w_in[:, d_inner + conv_dim:].astype(MXU_DTYPE)
    bias = dt_bias.reshape(1, -1)
    alog = a_log.reshape(1, -1)
    return dict(n_heads=n_heads, w_z=w_in[:, :d_inner].astype(MXU_DTYPE),
                w_xbc=w_in[:, d_inner:d_inner + conv_dim].astype(MXU_DTYPE), w_dt=w_dt, w_dt_t=w_dt.T,
                conv_w=conv_w, conv_b=_row(conv_b), bias=bias, bias_t=_lane_bcast(bias, SSD_STEP),
                alog=alog, alog_t=_lane_bcast(alog, CHUNK), d_skip=_row(jnp.repeat(d_skip, SSD_HEAD_DIM)),
                norm_g=_row(norm_g), w_out=w_out.astype(MXU_DTYPE))


def _ssd_layer(x, batch, g_pre, g_post, p):
    z, xbc, dt_raw, dt_raw_t = _norm_proj(x, g_pre, [p["w_z"], p["w_xbc"], p["w_dt"]], [p["w_dt_t"]])
    act, dtp, dtp_t, y_bwd = _ssd_bwd(xbc, dt_raw, dt_raw_t, p["conv_w"], p["conv_b"], p["bias"], p["bias_t"],
                                      p["alog"], p["alog_t"], batch, p["n_heads"])
    return _ssd_fwd(act, dtp, dtp_t, y_bwd, z, x, p["alog"], p["alog_t"], p["d_skip"], p["norm_g"],
                    p["w_out"], g_post, batch, p["n_heads"])


def _prep_attn(w_qkv, sink, w_out):
    q_dim = w_out.shape[0]
    kv_dim = (w_qkv.shape[1] - q_dim) // 2
    w_q_t = (w_qkv[:, :q_dim] * (LOG2_E * HEAD_DIM ** -0.5)).T.astype(MXU_DTYPE)
    return dict(w_q_t=w_q_t, w_k=w_qkv[:, q_dim:q_dim + kv_dim].astype(MXU_DTYPE),
                w_v_t=w_qkv[:, q_dim + kv_dim:].T.astype(MXU_DTYPE),
                sink_row=_row(jnp.repeat(sink * LOG2_E, BLOCK)), w_out=w_out.astype(MXU_DTYPE))


def _attn_layer(x, batch, g_pre, g_post, p, rel_bias):
    k, q_t, v_t = _norm_proj(x, g_pre, [p["w_k"]], [p["w_q_t"], p["w_v_t"]], dtypes=[MXU_DTYPE] * 3)
    return _attn(q_t, k, v_t, x, p["sink_row"], rel_bias, p["w_out"], g_post, batch)


def _encoder(x3, norm_g, ffn, mixers, rel_bias):
    batch, seq, d = x3.shape
    x = x3.reshape(batch * seq, d)
    for i, mixer in enumerate(mixers):
        ng = [_row(norm_g[i, s]) for s in range(norm_g.shape[1])]
        x = _ffn(x, ng[0], ng[1], *ffn[i][0])
        if "w_z" in mixer:
            x = _ssd_layer(x, batch, ng[2], ng[3], mixer)
        else:
            x = _attn_layer(x, batch, ng[2], ng[3], mixer, rel_bias)
        x = _ffn(x, ng[4], ng[5], *ffn[i][1])
    return x.reshape(batch, seq, d)


def kernel(x_prompt, x_sample, norm_g, ffn_w_gate, ffn_w_up, ffn_w_down, ssd_w_in, ssd_conv_w, ssd_conv_b,
           ssd_dt_bias, ssd_A_log, ssd_D, ssd_norm_g, ssd_w_out, attn_w_qkv, attn_sink, attn_w_out, rel_bias):
    depth = norm_g.shape[0]
    ffn = [[(ffn_w_gate[i, s].astype(MXU_DTYPE), ffn_w_up[i, s].astype(MXU_DTYPE),
             ffn_w_down[i, s].astype(MXU_DTYPE)) for s in range(2)] for i in range(depth)]
    mixers = []
    for i in range(depth):
        j = i // 2
        if i % 2 == 0:
            mixers.append(_prep_ssd(ssd_w_in[j], ssd_conv_w[j], ssd_conv_b[j], ssd_dt_bias[j], ssd_A_log[j],
                                    ssd_D[j], ssd_norm_g[j], ssd_w_out[j]))
        else:
            mixers.append(_prep_attn(attn_w_qkv[j], attn_sink[j], attn_w_out[j]))
    return tuple(_encoder(x, norm_g, ffn, mixers, rel_bias) for x in (x_prompt, x_sample))
```

```python
import functools

import jax
import jax.numpy as jnp
import numpy as np
from jax import lax
from jax.experimental import pallas as pl
from jax.experimental.pallas import tpu as pltpu

F32 = jnp.float32
MXU_DTYPE = jnp.bfloat16

EPS = 1e-6
SSD_HEAD_DIM = 64
N_GROUPS = 8
D_STATE = 128
CHUNK = 128
D_CONV = 5
CONV_PAD = D_CONV // 2
HALO = 8
BIAS_ROWS = 16
SSD_STEP = 4 * CHUNK
HEAD_DIM = 64
N_KV_HEADS = 4
BLOCK = 128
WINDOW = 128
N_BUCKETS = 32
FFN_TILE = 1024
FFN_SUBTILE = 256
PROJ_TILE = 512
ATTN_STEP = 4 * BLOCK
N_EDGE_VARIANTS = 4

V7X_VMEM_BYTES = 64 * 1024 * 1024
VMEM_LIMIT_BYTES = V7X_VMEM_BYTES - 8 * 1024 * 1024

NEG_INF = float("-inf")
LOG2_E = 1.4426950408889634


def _rmsnorm(x, g):
    return x * lax.rsqrt(jnp.mean(x * x, axis=-1, keepdims=True) + EPS) * g


def _silu(x):
    t = 0.5 * x
    return t + t * jnp.tanh(t)


def _mm(a, b):
    return jnp.dot(a.astype(MXU_DTYPE), b.astype(MXU_DTYPE), preferred_element_type=F32)


def _mm_nt(a, b):
    return lax.dot_general(a.astype(MXU_DTYPE), b.astype(MXU_DTYPE), (((1,), (1,)), ((), ())),
                           preferred_element_type=F32)


def _mm_f32(a, b):
    return jnp.dot(a, b, precision=lax.Precision.HIGHEST, preferred_element_type=F32)


def _const_spec(shape):
    zeros = (0,) * len(shape)
    return pl.BlockSpec(shape, lambda *_: zeros, pipeline_mode=pl.Buffered(1))


def _params(n_axes):
    return pltpu.CompilerParams(dimension_semantics=("arbitrary",) * n_axes,
                                vmem_limit_bytes=VMEM_LIMIT_BYTES)


def _ffn_body(x_ref, gpre_ref, gpost_ref, wg_ref, wu_ref, wd_ref, o_ref):
    for s in range(x_ref.shape[0] // FFN_SUBTILE):
        rows = slice(s * FFN_SUBTILE, (s + 1) * FFN_SUBTILE)
        x = x_ref[rows, :]
        h = _rmsnorm(x, gpre_ref[...]).astype(MXU_DTYPE)
        g = jnp.dot(h, wg_ref[...], preferred_element_type=F32)
        u = jnp.dot(h, wu_ref[...], preferred_element_type=F32)
        a = (_silu(g) * u).astype(MXU_DTYPE)
        y = jnp.dot(a, wd_ref[...], preferred_element_type=F32)
        o_ref[rows, :] = x + 0.5 * _rmsnorm(y, gpost_ref[...])


def _ffn(x, g_pre, g_post, wg, wu, wd):
    m, d = x.shape
    f = wg.shape[1]
    tm = FFN_TILE
    assert m % tm == 0 and tm % FFN_SUBTILE == 0
    row = pl.BlockSpec((tm, d), lambda i: (i, 0))
    return pl.pallas_call(
        _ffn_body,
        grid=(m // tm,),
        in_specs=[row, _const_spec((1, d)), _const_spec((1, d)),
                  _const_spec((d, f)), _const_spec((d, f)), _const_spec((f, d))],
        out_specs=row,
        out_shape=jax.ShapeDtypeStruct((m, d), x.dtype),
        compiler_params=_params(1),
        name="ffn",
    )(x, g_pre, g_post, wg, wu, wd)


def _norm_proj_body(n_w, n_wt, x_ref, g_ref, *refs):
    w_refs = refs[:n_w]
    wt_refs = refs[n_w:n_w + n_wt]
    o_refs = refs[n_w + n_wt:]
    h = _rmsnorm(x_ref[...], g_ref[...]).astype(MXU_DTYPE)
    for w_ref, o_ref in zip(w_refs, o_refs[:n_w]):
        o_ref[...] = jnp.dot(h, w_ref[...], preferred_element_type=F32).astype(o_ref.dtype)
    for wt_ref, o_ref in zip(wt_refs, o_refs[n_w:]):
        o_ref[...] = _mm_nt(wt_ref[...], h).astype(o_ref.dtype)


def _norm_proj(x, g, ws, wts, dtypes=None):
    m, d = x.shape
    tm = PROJ_TILE
    assert m % tm == 0
    dtypes = dtypes or [F32] * (len(ws) + len(wts))
    row = lambda n: pl.BlockSpec((tm, n), lambda i: (i, 0))
    col = lambda n: pl.BlockSpec((n, tm), lambda i: (0, i))
    return pl.pallas_call(
        functools.partial(_norm_proj_body, len(ws), len(wts)),
        grid=(m // tm,),
        in_specs=[row(d), _const_spec((1, d))] + [_const_spec(w.shape) for w in ws]
                 + [_const_spec(w.shape) for w in wts],
        out_specs=[row(w.shape[1]) for w in ws] + [col(w.shape[0]) for w in wts],
        out_shape=[jax.ShapeDtypeStruct((m, w.shape[1]), t) for w, t in zip(ws, dtypes)]
                  + [jax.ShapeDtypeStruct((w.shape[0], m), t) for w, t in zip(wts, dtypes[len(ws):])],
        compiler_params=_params(1),
        name="norm_proj",
    )(x, g, *ws, *wts)


def _scan_chunk(xs, bm, cm, dt, dt_t, alog, alog_t, state_ref, y_ref, reverse):
    length = xs.shape[0]
    n_heads = dt.shape[1]
    heads_per_group = n_heads // N_GROUPS
    group_width = heads_per_group * SSD_HEAD_DIM
    assert group_width % 128 == 0 and 128 % SSD_HEAD_DIM == 0
    heads_per_tile = 128 // SSD_HEAD_DIM
    row = lax.broadcasted_iota(jnp.int32, (length, length), 0)
    col = lax.broadcasted_iota(jnp.int32, (length, length), 1)
    valid = (col >= row) if reverse else (col <= row)
    valid_t = (col <= row) if reverse else (col >= row)
    a = dt * (-LOG2_E * jnp.exp(alog))
    a_t = dt_t * (-LOG2_E * jnp.exp(alog_t))
    acs = _mm_f32(valid.astype(F32), a)
    acs_t = _mm_f32(a_t, valid_t.astype(F32))
    tot_t = _mm_f32(a_t, jnp.ones((length, length), F32))
    end = 0 if reverse else length - 1
    w_t = dt_t * jnp.exp2(tot_t - acs_t)
    src_t = acs_t - jnp.log2(dt_t)

    stack_row = lax.broadcasted_iota(jnp.int32, (heads_per_group * length, group_width), 0)
    stack_col = lax.broadcasted_iota(jnp.int32, (heads_per_group * length, group_width), 1)
    block_diag = (stack_row // length) == (stack_col // SSD_HEAD_DIM)
    head_in_tile = lax.broadcasted_iota(jnp.int32, (length, 128), 1) // SSD_HEAD_DIM

    for g in range(N_GROUPS):
        bg = bm[:, g * D_STATE:(g + 1) * D_STATE]
        cg = cm[:, g * D_STATE:(g + 1) * D_STATE].astype(MXU_DTYPE)
        bg_t = bg.T
        prev = state_ref[g]
        c_both = _mm(cg, jnp.concatenate([bg_t.astype(MXU_DTYPE), prev.astype(MXU_DTYPE)], axis=1))
        cb = c_both[:, :length]
        lhs_y, lhs_s, acs_cols = [], [], []
        for r in range(heads_per_group):
            h = g * heads_per_group + r
            acs_col = jnp.broadcast_to(acs[:, h:h + 1], (length, length))
            seg_dt = jnp.exp2(jnp.where(valid, acs_col - src_t[h:h + 1, :], NEG_INF))
            lhs_y.append((cb * seg_dt).astype(MXU_DTYPE))
            lhs_s.append((bg_t * w_t[h:h + 1, :]).astype(MXU_DTYPE))
            acs_cols.append(acs_col)
        lanes = slice(g * group_width, (g + 1) * group_width)
        x_g = xs[:, lanes].astype(MXU_DTYPE)
        rhs = jnp.where(block_diag, jnp.concatenate([x_g] * heads_per_group, axis=0), jnp.zeros((), MXU_DTYPE))
        y_diag = jnp.dot(jnp.concatenate(lhs_y, axis=1), rhs, preferred_element_type=F32)
        new = jnp.dot(jnp.concatenate(lhs_s, axis=1), rhs, preferred_element_type=F32)
        tiles = []
        for t in range(group_width // 128):
            tile = acs_cols[t * heads_per_tile]
            for p in range(1, heads_per_tile):
                tile = jnp.where(head_in_tile == p, acs_cols[t * heads_per_tile + p], tile)
            tiles.append(tile)
        off = jnp.exp2(jnp.concatenate(tiles, axis=1))
        y_ref[:, lanes] = y_diag + c_both[:, length:] * off
        state_ref[g] = prev * off[end:end + 1, :] + new


def _ssd_bwd_body(n_heads, xbc_ref, prev_ref, next_ref, dt_ref, dtt_ref, sel_ref, convw_ref, convb_ref,
                  bias_ref, biast_ref, alog_ref, alogt_ref,
                  act_ref, dtp_ref, dtpt_ref, y_ref, state_ref):
    i = pl.program_id(1)
    n_steps = pl.num_programs(1)
    step = n_steps - 1 - i
    d_inner = n_heads * SSD_HEAD_DIM
    gn = N_GROUPS * D_STATE

    @pl.when(i == 0)
    def _():
        state_ref[...] = jnp.zeros_like(state_ref)

    dtp = jax.nn.softplus(dt_ref[...] + bias_ref[...])
    dtp_t = jax.nn.softplus(dtt_ref[...] + biast_ref[...])
    dtp_ref[...] = dtp
    dtpt_ref[...] = dtp_t

    ext_all = jnp.concatenate([jnp.where(step > 0, prev_ref[...], 0.0), xbc_ref[...],
                               jnp.where(step < n_steps - 1, next_ref[...], 0.0)], axis=0)
    ext_all = ext_all.astype(MXU_DTYPE)
    conv_w = convw_ref[...]
    for s in reversed(range(xbc_ref.shape[0] // CHUNK)):
        rows = slice(s * CHUNK, (s + 1) * CHUNK)
        ext = ext_all[s * CHUNK:(s + 1) * CHUNK + 2 * HALO]
        taps = jnp.concatenate([ext * conv_w[k:k + 1, :] for k in range(D_CONV)] + [convb_ref[...]], axis=0)
        half = jnp.dot(sel_ref[...], taps, preferred_element_type=F32)
        act = half + half * jnp.tanh(half)
        act_ref[rows, :] = act
        _scan_chunk(act[:, :d_inner], act[:, d_inner:d_inner + gn], act[:, d_inner + gn:],
                    dtp[rows, n_heads:], dtp_t[n_heads:, rows], alog_ref[:, n_heads:], alogt_ref[n_heads:, :],
                    state_ref, y_ref.at[rows], reverse=True)


def _conv_select():
    ext = CHUNK + 2 * HALO
    t = np.arange(CHUNK)[:, None]
    j = np.arange(D_CONV * ext + BIAS_ROWS)[None, :]
    taps = (j < D_CONV * ext) & ((j % ext) == t + HALO - CONV_PAD + j // ext)
    bias = (j >= D_CONV * ext) & (j < D_CONV * ext + 2)
    return jnp.asarray(taps | bias, dtype=MXU_DTYPE)


def _conv_bias_rows(conv_b):
    half = 0.5 * conv_b
    hi = half.astype(MXU_DTYPE)
    lo = (half - hi.astype(F32)).astype(MXU_DTYPE)
    return jnp.concatenate([hi[None], lo[None], jnp.zeros((BIAS_ROWS - 2, conv_b.shape[0]), MXU_DTYPE)], axis=0)


def _ssd_bwd(xbc, dt_raw, dt_raw_t, conv_w, conv_b, dt_bias, dt_bias_t, alog, alog_t, batch, n_heads):
    m, c = xbc.shape
    seq = m // batch
    assert seq % SSD_STEP == 0
    nc = seq // SSD_STEP
    hb = SSD_STEP // HALO
    d_inner = n_heads * SSD_HEAD_DIM
    sel = _conv_select()
    rev = lambda b, i: b * nc + (nc - 1 - i)
    row = lambda n: pl.BlockSpec((SSD_STEP, n), lambda b, i: (rev(b, i), 0))
    col = lambda n: pl.BlockSpec((n, SSD_STEP), lambda b, i: (0, rev(b, i)))
    prev = pl.BlockSpec((HALO, c), lambda b, i: (jnp.maximum(rev(b, i) * hb - 1, 0), 0))
    nxt = pl.BlockSpec((HALO, c), lambda b, i: (jnp.minimum((rev(b, i) + 1) * hb, m // HALO - 1), 0))
    return pl.pallas_call(
        functools.partial(_ssd_bwd_body, n_heads),
        grid=(batch, nc),
        in_specs=[row(c), prev, nxt, row(2 * n_heads), col(2 * n_heads),
                  _const_spec(sel.shape), _const_spec(conv_w.shape), _const_spec(conv_b.shape),
                  _const_spec(dt_bias.shape), _const_spec(dt_bias_t.shape),
                  _const_spec(alog.shape), _const_spec(alog_t.shape)],
        out_specs=[row(c), row(2 * n_heads), col(2 * n_heads), row(d_inner)],
        out_shape=[jax.ShapeDtypeStruct((m, c), F32),
                   jax.ShapeDtypeStruct((m, 2 * n_heads), F32),
                   jax.ShapeDtypeStruct((2 * n_heads, m), F32),
                   jax.ShapeDtypeStruct((m, d_inner), F32)],
        scratch_shapes=[pltpu.VMEM((N_GROUPS, D_STATE, d_inner // N_GROUPS), F32)],
        compiler_params=_params(2),
        name="ssd_bwd",
    )(xbc, xbc, xbc, dt_raw, dt_raw_t, sel, conv_w, conv_b, dt_bias, dt_bias_t, alog, alog_t)


def _ssd_fwd_body(n_heads, act_ref, dtp_ref, dtpt_ref, ybwd_ref, z_ref, x_ref, alog_ref, alogt_ref,
                  dskip_ref, normg_ref, wout_ref, gpost_ref, o_ref, y_ref, state_ref):
    d_inner = n_heads * SSD_HEAD_DIM
    gn = N_GROUPS * D_STATE
    group_width = d_inner // N_GROUPS

    @pl.when(pl.program_id(1) == 0)
    def _():
        state_ref[...] = jnp.zeros_like(state_ref)

    for s in range(act_ref.shape[0] // CHUNK):
        rows = slice(s * CHUNK, (s + 1) * CHUNK)
        _scan_chunk(act_ref[rows, :d_inner], act_ref[rows, d_inner:d_inner + gn], act_ref[rows, d_inner + gn:],
                    dtp_ref[rows, :n_heads], dtpt_ref[:n_heads, rows], alog_ref[:, :n_heads],
                    alogt_ref[:n_heads, :], state_ref, y_ref.at[rows], reverse=False)

    zh = z_ref[...]
    y = (y_ref[...] + ybwd_ref[...] + act_ref[:, :d_inner] * dskip_ref[...]) * (zh + zh * jnp.tanh(zh))
    for g in range(N_GROUPS):
        lanes = slice(g * group_width, (g + 1) * group_width)
        yg = y[:, lanes]
        yg = yg * lax.rsqrt(jnp.mean(yg * yg, axis=-1, keepdims=True) + EPS)
        y_ref[:, lanes] = yg * normg_ref[:, lanes]
    mix = jnp.dot(y_ref[...].astype(MXU_DTYPE), wout_ref[...], preferred_element_type=F32)
    o_ref[...] = x_ref[...] + _rmsnorm(mix, gpost_ref[...])


def _ssd_fwd(act, dtp, dtp_t, y_bwd, z, x, alog, alog_t, d_skip, norm_g, w_out, g_post, batch, n_heads):
    m, c = act.shape
    d = x.shape[1]
    seq = m // batch
    assert seq % SSD_STEP == 0
    nc = seq // SSD_STEP
    d_inner = n_heads * SSD_HEAD_DIM
    row = lambda n: pl.BlockSpec((SSD_STEP, n), lambda b, i: (b * nc + i, 0))
    col = lambda n: pl.BlockSpec((n, SSD_STEP), lambda b, i: (0, b * nc + i))
    return pl.pallas_call(
        functools.partial(_ssd_fwd_body, n_heads),
        grid=(batch, nc),
        in_specs=[row(c), row(2 * n_heads), col(2 * n_heads), row(d_inner), row(d_inner), row(d),
                  _const_spec(alog.shape), _const_spec(alog_t.shape), _const_spec(d_skip.shape),
                  _const_spec(norm_g.shape), _const_spec(w_out.shape), _const_spec(g_post.shape)],
        out_specs=row(d),
        out_shape=jax.ShapeDtypeStruct((m, d), F32),
        scratch_shapes=[pltpu.VMEM((SSD_STEP, d_inner), F32),
                        pltpu.VMEM((N_GROUPS, D_STATE, d_inner // N_GROUPS), F32)],
        compiler_params=_params(2),
        name="ssd_fwd",
    )(act, dtp, dtp_t, y_bwd, z, x, alog, alog_t, d_skip, norm_g, w_out, g_post)


def _t5_bias_table(relb_ref, bias_ref, group):
    kj = lax.broadcasted_iota(jnp.int32, (3 * BLOCK, BLOCK), 0)
    qi = lax.broadcasted_iota(jnp.int32, (3 * BLOCK, BLOCK), 1)
    rel = kj - BLOCK - qi
    n = jnp.abs(rel)
    half = N_BUCKETS // 2
    max_exact = half // 2
    large = jnp.full_like(n, max_exact)
    for k in range(1, half - max_exact):
        large = large + (n * n >= (max_exact * max_exact) << k).astype(jnp.int32)
    bucket = jnp.where(rel > 0, half, 0) + jnp.where(n < max_exact, n, large)
    in_window = n <= WINDOW
    n_kv = bias_ref.shape[0] // N_EDGE_VARIANTS
    for kv in range(n_kv):
        for r in range(group):
            acc = jnp.zeros((3 * BLOCK, BLOCK), F32)
            for b in range(N_BUCKETS):
                acc = jnp.where(bucket == b, relb_ref[b, kv * group + r] * LOG2_E, acc)
            acc = jnp.where(in_window, acc, NEG_INF)
            for variant in range(N_EDGE_VARIANTS):
                tbl = acc
                if variant & 1:
                    tbl = jnp.where(kj >= BLOCK, tbl, NEG_INF)
                if variant & 2:
                    tbl = jnp.where(kj < 2 * BLOCK, tbl, NEG_INF)
                bias_ref[variant * n_kv + kv, :, r * BLOCK:(r + 1) * BLOCK] = tbl


def _attn_body(qt_ref, kp_ref, kc_ref, kn_ref, vtp_ref, vtc_ref, vtn_ref, x_ref, sink_ref, relb_ref,
               wout_ref, gpost_ref, o_ref, bias_ref, ctx_ref):
    step = pl.program_id(1)
    blocks_per_step = qt_ref.shape[1] // BLOCK
    last_block = pl.num_programs(1) * blocks_per_step - 1
    n_kv = bias_ref.shape[0] // N_EDGE_VARIANTS
    group = qt_ref.shape[0] // HEAD_DIM // n_kv
    width = group * BLOCK

    @pl.when((pl.program_id(0) == 0) & (step == 0))
    def _():
        _t5_bias_table(relb_ref, bias_ref, group)

    kb_all = jnp.concatenate([kp_ref[...], kc_ref[...], kn_ref[...]], axis=0)
    vtb_all = jnp.concatenate([vtp_ref[...], vtc_ref[...], vtn_ref[...]], axis=1)
    heads_per_lane_tile = 128 // HEAD_DIM
    for sub in range(blocks_per_step):
        blk = step * blocks_per_step + sub
        variant = (blk == 0).astype(jnp.int32) + 2 * (blk == last_block).astype(jnp.int32)
        kb = kb_all[sub * BLOCK:(sub + 3) * BLOCK]
        vtb = vtb_all[:, sub * BLOCK:(sub + 3) * BLOCK]
        q_lanes = slice(sub * BLOCK, (sub + 1) * BLOCK)
        for kv in range(n_kv):
            tile = kv // heads_per_lane_tile
            k_tile = kb[:, tile * 128:(tile + 1) * 128]
            q_g = jnp.concatenate([qt_ref[(kv * group + r) * HEAD_DIM:(kv * group + r + 1) * HEAD_DIM, q_lanes]
                                   for r in range(group)], axis=1)
            q_pad = jnp.concatenate([q_g if p == kv % heads_per_lane_tile else jnp.zeros_like(q_g)
                                     for p in range(heads_per_lane_tile)], axis=0)
            scores = jnp.dot(k_tile, q_pad, preferred_element_type=F32)
            scores = scores + bias_ref[variant * n_kv + kv]
            sink = sink_ref[:, kv * width:(kv + 1) * width]
            mx = jnp.maximum(jnp.max(scores, axis=0, keepdims=True), sink)
            e = jnp.exp2(scores - mx)
            denom = jnp.sum(e, axis=0, keepdims=True) + jnp.exp2(sink - mx)
            vt_h = vtb[kv * HEAD_DIM:(kv + 1) * HEAD_DIM, :]
            ctx_t = jnp.dot(vt_h, e.astype(MXU_DTYPE), preferred_element_type=F32) / denom
            for r in range(group):
                h = kv * group + r
                ctx_ref[h * HEAD_DIM:(h + 1) * HEAD_DIM, q_lanes] = ctx_t[:, r * BLOCK:(r + 1) * BLOCK]
    ctx = ctx_ref[...].T
    mix = jnp.dot(ctx.astype(MXU_DTYPE), wout_ref[...], preferred_element_type=F32)
    o_ref[...] = x_ref[...] + _rmsnorm(mix, gpost_ref[...])


def _attn(q_t, k, v_t, x, sink_row, rel_bias, w_out, g_post, batch):
    qd, m = q_t.shape
    kvd = k.shape[1]
    d = x.shape[1]
    n_kv = kvd // HEAD_DIM
    group = qd // kvd
    assert (m // batch) % ATTN_STEP == 0
    nb = m // batch // BLOCK
    bps = ATTN_STEP // BLOCK
    ns = nb // bps
    cur_i = lambda b, i: b * ns + i
    prev_i = lambda b, i: b * nb + jnp.maximum(i * bps - 1, 0)
    next_i = lambda b, i: b * nb + jnp.minimum((i + 1) * bps, nb - 1)
    rows = lambda size, n, f: pl.BlockSpec((size, n), lambda b, i: (f(b, i), 0))
    cols = lambda size, n, f: pl.BlockSpec((n, size), lambda b, i: (0, f(b, i)))
    smem = pl.BlockSpec(memory_space=pltpu.SMEM)
    return pl.pallas_call(
        _attn_body,
        grid=(batch, ns),
        in_specs=[cols(ATTN_STEP, qd, cur_i),
                  rows(BLOCK, kvd, prev_i), rows(ATTN_STEP, kvd, cur_i), rows(BLOCK, kvd, next_i),
                  cols(BLOCK, kvd, prev_i), cols(ATTN_STEP, kvd, cur_i), cols(BLOCK, kvd, next_i),
                  rows(ATTN_STEP, d, cur_i),
                  _const_spec(sink_row.shape), smem, _const_spec(w_out.shape), _const_spec(g_post.shape)],
        out_specs=rows(ATTN_STEP, d, cur_i),
        out_shape=jax.ShapeDtypeStruct((m, d), F32),
        scratch_shapes=[pltpu.VMEM((N_EDGE_VARIANTS * n_kv, 3 * BLOCK, group * BLOCK), F32),
                        pltpu.VMEM((qd, ATTN_STEP), F32)],
        compiler_params=_params(2),
        name="attn",
    )(q_t, k, k, k, v_t, v_t, v_t, x, sink_row, rel_bias, w_out, g_post)


def _row(v):
    return v.reshape(1, -1)


def _lane_bcast(r, width):
    return jnp.broadcast_to(r.reshape(-1, 1), (r.shape[1], width))


def _prep_ssd(w_in, conv_w, conv_b, dt_bias, a_log, d_skip, norm_g, w_out):
    n_heads = d_skip.shape[0]
    d_inner = n_heads * SSD_HEAD_DIM
    conv_dim = conv_w.shape[1]
    w_dt = w_in[:, d_inner + conv_dim:].astype(MXU_DTYPE)
    bias = dt_bias.reshape(1, -1)
    alog = a_log.reshape(1, -1)
    return dict(n_heads=n_heads, w_z=(0.5 * w_in[:, :d_inner]).astype(MXU_DTYPE),
                w_xbc=w_in[:, d_inner:d_inner + conv_dim].astype(MXU_DTYPE), w_dt=w_dt, w_dt_t=w_dt.T,
                conv_w=(0.5 * conv_w).astype(MXU_DTYPE), conv_b=_conv_bias_rows(conv_b),
                bias=bias, bias_t=_lane_bcast(bias, SSD_STEP),
                alog=alog, alog_t=_lane_bcast(alog, CHUNK), d_skip=_row(jnp.repeat(d_skip, SSD_HEAD_DIM)),
                norm_g=_row(norm_g), w_out=w_out.astype(MXU_DTYPE))


def _ssd_layer(x, batch, g_pre, g_post, p):
    z, xbc, dt_raw, dt_raw_t = _norm_proj(x, g_pre, [p["w_z"], p["w_xbc"], p["w_dt"]], [p["w_dt_t"]])
    act, dtp, dtp_t, y_bwd = _ssd_bwd(xbc, dt_raw, dt_raw_t, p["conv_w"], p["conv_b"], p["bias"], p["bias_t"],
                                      p["alog"], p["alog_t"], batch, p["n_heads"])
    return _ssd_fwd(act, dtp, dtp_t, y_bwd, z, x, p["alog"], p["alog_t"], p["d_skip"], p["norm_g"],
                    p["w_out"], g_post, batch, p["n_heads"])


def _prep_attn(w_qkv, sink, w_out):
    q_dim = w_out.shape[0]
    kv_dim = (w_qkv.shape[1] - q_dim) // 2
    w_q_t = (w_qkv[:, :q_dim] * (LOG2_E * HEAD_DIM ** -0.5)).T.astype(MXU_DTYPE)
    return dict(w_q_t=w_q_t, w_k=w_qkv[:, q_dim:q_dim + kv_dim].astype(MXU_DTYPE),
                w_v_t=w_qkv[:, q_dim + kv_dim:].T.astype(MXU_DTYPE),
                sink_row=_row(jnp.repeat(sink * LOG2_E, BLOCK)), w_out=w_out.astype(MXU_DTYPE))


def _attn_layer(x, batch, g_pre, g_post, p, rel_bias):
    k, q_t, v_t = _norm_proj(x, g_pre, [p["w_k"]], [p["w_q_t"], p["w_v_t"]], dtypes=[MXU_DTYPE] * 3)
    return _attn(q_t, k, v_t, x, p["sink_row"], rel_bias, p["w_out"], g_post, batch)


def _encoder(x3, norm_g, ffn, mixers, rel_bias):
    batch, seq, d = x3.shape
    x = x3.reshape(batch * seq, d)
    for i, mixer in enumerate(mixers):
        ng = [_row(norm_g[i, s]) for s in range(norm_g.shape[1])]
        x = _ffn(x, ng[0], ng[1], *ffn[i][0])
        if "w_z" in mixer:
            x = _ssd_layer(x, batch, ng[2], ng[3], mixer)
        else:
            x = _attn_layer(x, batch, ng[2], ng[3], mixer, rel_bias)
        x = _ffn(x, ng[4], ng[5], *ffn[i][1])
    return x.reshape(batch, seq, d)


def kernel(x_prompt, x_sample, norm_g, ffn_w_gate, ffn_w_up, ffn_w_down, ssd_w_in, ssd_conv_w, ssd_conv_b,
           ssd_dt_bias, ssd_A_log, ssd_D, ssd_norm_g, ssd_w_out, attn_w_qkv, attn_sink, attn_w_out, rel_bias):
    depth = norm_g.shape[0]
    ffn = [[(ffn_w_gate[i, s].astype(MXU_DTYPE), ffn_w_up[i, s].astype(MXU_DTYPE),
             ffn_w_down[i, s].astype(MXU_DTYPE)) for s in range(2)] for i in range(depth)]
    mixers = []
    for i in range(depth):
        j = i // 2
        if i % 2 == 0:
            mixers.append(_prep_ssd(ssd_w_in[j], ssd_conv_w[j], ssd_conv_b[j], ssd_dt_bias[j], ssd_A_log[j],
                                    ssd_D[j], ssd_norm_g[j], ssd_w_out[j]))
        else:
            mixers.append(_prep_attn(attn_w_qkv[j], attn_sink[j], attn_w_out[j]))
    return tuple(_encoder(x, norm_g, ffn, mixers, rel_bias) for x in (x_prompt, x_sample))
```

```python
import functools

import jax
import jax.numpy as jnp
import numpy as np
from jax import lax
from jax.experimental import pallas as pl
from jax.experimental.pallas import tpu as pltpu

F32 = jnp.float32
MXU_DTYPE = jnp.bfloat16

EPS = 1e-6
SSD_HEAD_DIM = 64
N_GROUPS = 8
D_STATE = 128
CHUNK = 128
D_CONV = 5
CONV_PAD = D_CONV // 2
HALO = 8
BIAS_ROWS = 16
SSD_STEP = 4 * CHUNK
HEAD_DIM = 64
N_KV_HEADS = 4
BLOCK = 128
WINDOW = 128
N_BUCKETS = 32
FFN_TILE = 1024
FFN_SUBTILE = 256
PROJ_TILE = 512
ATTN_STEP = 4 * BLOCK
N_EDGE_VARIANTS = 4

V7X_VMEM_BYTES = 64 * 1024 * 1024
VMEM_LIMIT_BYTES = V7X_VMEM_BYTES - 8 * 1024 * 1024

NEG_INF = float("-inf")
LOG2_E = 1.4426950408889634


def _rmsnorm(x, g):
    return x * lax.rsqrt(jnp.mean(x * x, axis=-1, keepdims=True) + EPS) * g


def _silu(x):
    t = 0.5 * x
    return t + t * jnp.tanh(t)


def _mm(a, b):
    return jnp.dot(a.astype(MXU_DTYPE), b.astype(MXU_DTYPE), preferred_element_type=F32)


def _mm_nt(a, b):
    return lax.dot_general(a.astype(MXU_DTYPE), b.astype(MXU_DTYPE), (((1,), (1,)), ((), ())),
                           preferred_element_type=F32)


def _mm_f32(a, b):
    return jnp.dot(a, b, precision=lax.Precision.HIGHEST, preferred_element_type=F32)


def _const_spec(shape):
    zeros = (0,) * len(shape)
    return pl.BlockSpec(shape, lambda *_: zeros, pipeline_mode=pl.Buffered(1))


def _params(n_axes):
    return pltpu.CompilerParams(dimension_semantics=("arbitrary",) * n_axes,
                                vmem_limit_bytes=VMEM_LIMIT_BYTES)


def _ffn_body(x_ref, gpre_ref, gpost_ref, wg_ref, wu_ref, wd_ref, o_ref):
    for s in range(x_ref.shape[0] // FFN_SUBTILE):
        rows = slice(s * FFN_SUBTILE, (s + 1) * FFN_SUBTILE)
        x = x_ref[rows, :]
        h = _rmsnorm(x, gpre_ref[...]).astype(MXU_DTYPE)
        g = jnp.dot(h, wg_ref[...], preferred_element_type=F32)
        u = jnp.dot(h, wu_ref[...], preferred_element_type=F32)
        a = (_silu(g) * u).astype(MXU_DTYPE)
        y = jnp.dot(a, wd_ref[...], preferred_element_type=F32)
        o_ref[rows, :] = x + 0.5 * _rmsnorm(y, gpost_ref[...])


def _ffn(x, g_pre, g_post, wg, wu, wd):
    m, d = x.shape
    f = wg.shape[1]
    tm = FFN_TILE
    assert m % tm == 0 and tm % FFN_SUBTILE == 0
    row = pl.BlockSpec((tm, d), lambda i: (i, 0))
    return pl.pallas_call(
        _ffn_body,
        grid=(m // tm,),
        in_specs=[row, _const_spec((1, d)), _const_spec((1, d)),
                  _const_spec((d, f)), _const_spec((d, f)), _const_spec((f, d))],
        out_specs=row,
        out_shape=jax.ShapeDtypeStruct((m, d), x.dtype),
        compiler_params=_params(1),
        name="ffn",
    )(x, g_pre, g_post, wg, wu, wd)


def _norm_proj_body(n_w, n_wt, x_ref, g_ref, *refs):
    w_refs = refs[:n_w]
    wt_refs = refs[n_w:n_w + n_wt]
    o_refs = refs[n_w + n_wt:]
    h = _rmsnorm(x_ref[...], g_ref[...]).astype(MXU_DTYPE)
    for w_ref, o_ref in zip(w_refs, o_refs[:n_w]):
        o_ref[...] = jnp.dot(h, w_ref[...], preferred_element_type=F32).astype(o_ref.dtype)
    for wt_ref, o_ref in zip(wt_refs, o_refs[n_w:]):
        o_ref[...] = _mm_nt(wt_ref[...], h).astype(o_ref.dtype)


def _norm_proj(x, g, ws, wts, dtypes=None):
    m, d = x.shape
    tm = PROJ_TILE
    assert m % tm == 0
    dtypes = dtypes or [F32] * (len(ws) + len(wts))
    row = lambda n: pl.BlockSpec((tm, n), lambda i: (i, 0))
    col = lambda n: pl.BlockSpec((n, tm), lambda i: (0, i))
    return pl.pallas_call(
        functools.partial(_norm_proj_body, len(ws), len(wts)),
        grid=(m // tm,),
        in_specs=[row(d), _const_spec((1, d))] + [_const_spec(w.shape) for w in ws]
                 + [_const_spec(w.shape) for w in wts],
        out_specs=[row(w.shape[1]) for w in ws] + [col(w.shape[0]) for w in wts],
        out_shape=[jax.ShapeDtypeStruct((m, w.shape[1]), t) for w, t in zip(ws, dtypes)]
                  + [jax.ShapeDtypeStruct((w.shape[0], m), t) for w, t in zip(wts, dtypes[len(ws):])],
        compiler_params=_params(1),
        name="norm_proj",
    )(x, g, *ws, *wts)


def _decay_tables(dt, dt_t, alog, alog_t, reverse):
    n_chunks = dt.shape[0] // CHUNK
    row = lax.broadcasted_iota(jnp.int32, (CHUNK, CHUNK), 0)
    col = lax.broadcasted_iota(jnp.int32, (CHUNK, CHUNK), 1)
    valid = (col >= row) if reverse else (col <= row)
    valid_t = (col <= row) if reverse else (col >= row)
    a = dt * (-LOG2_E * jnp.exp(alog))
    a_t = dt_t * (-LOG2_E * jnp.exp(alog_t))
    chunks = [slice(c * CHUNK, (c + 1) * CHUNK) for c in range(n_chunks)]
    a_cols = jnp.concatenate([a[c] for c in chunks], axis=1)
    a_rows = jnp.concatenate([a_t[:, c] for c in chunks], axis=0)
    dt_rows = jnp.concatenate([dt_t[:, c] for c in chunks], axis=0)
    acs = _mm_f32(valid.astype(F32), a_cols)
    acs_t = _mm_f32(a_rows, valid_t.astype(F32))
    tot_t = _mm_f32(a_rows, jnp.ones((CHUNK, CHUNK), F32))
    w_t = dt_rows * jnp.exp2(tot_t - acs_t)
    src_t = acs_t - jnp.log2(dt_rows)
    return valid, acs, src_t, w_t


def _scan_chunk(xs, bm, cm, tables, base, n_heads, state_ref, y_ref, reverse):
    valid, acs, src_t, w_t = tables
    length = xs.shape[0]
    heads_per_group = n_heads // N_GROUPS
    group_width = heads_per_group * SSD_HEAD_DIM
    assert group_width % 128 == 0 and 128 % SSD_HEAD_DIM == 0
    heads_per_tile = 128 // SSD_HEAD_DIM
    end = 0 if reverse else length - 1

    stack_row = lax.broadcasted_iota(jnp.int32, (heads_per_group * length, group_width), 0)
    stack_col = lax.broadcasted_iota(jnp.int32, (heads_per_group * length, group_width), 1)
    block_diag = (stack_row // length) == (stack_col // SSD_HEAD_DIM)
    head_in_tile = lax.broadcasted_iota(jnp.int32, (length, 128), 1) // SSD_HEAD_DIM

    for g in range(N_GROUPS):
        bg = bm[:, g * D_STATE:(g + 1) * D_STATE]
        cg = cm[:, g * D_STATE:(g + 1) * D_STATE].astype(MXU_DTYPE)
        bg_t = bg.T
        prev = state_ref[g]
        c_both = _mm(cg, jnp.concatenate([bg_t.astype(MXU_DTYPE), prev.astype(MXU_DTYPE)], axis=1))
        cb = c_both[:, :length]
        lhs_y, lhs_s, acs_cols = [], [], []
        for r in range(heads_per_group):
            h = base + g * heads_per_group + r
            acs_col = jnp.broadcast_to(acs[:, h:h + 1], (length, length))
            seg_dt = jnp.exp2(jnp.where(valid, acs_col - src_t[h:h + 1, :], NEG_INF))
            lhs_y.append((cb * seg_dt).astype(MXU_DTYPE))
            lhs_s.append((bg_t * w_t[h:h + 1, :]).astype(MXU_DTYPE))
            acs_cols.append(acs_col)
        lanes = slice(g * group_width, (g + 1) * group_width)
        x_g = xs[:, lanes].astype(MXU_DTYPE)
        rhs = jnp.where(block_diag, jnp.concatenate([x_g] * heads_per_group, axis=0), jnp.zeros((), MXU_DTYPE))
        y_diag = jnp.dot(jnp.concatenate(lhs_y, axis=1), rhs, preferred_element_type=F32)
        new = jnp.dot(jnp.concatenate(lhs_s, axis=1), rhs, preferred_element_type=F32)
        tiles = []
        for t in range(group_width // 128):
            tile = acs_cols[t * heads_per_tile]
            for p in range(1, heads_per_tile):
                tile = jnp.where(head_in_tile == p, acs_cols[t * heads_per_tile + p], tile)
            tiles.append(tile)
        off = jnp.exp2(jnp.concatenate(tiles, axis=1))
        y_ref[:, lanes] = y_diag + c_both[:, length:] * off
        state_ref[g] = prev * off[end:end + 1, :] + new


def _ssd_bwd_body(n_heads, xbc_ref, prev_ref, next_ref, dt_ref, dtt_ref, sel_ref, convw_ref, convb_ref,
                  bias_ref, biast_ref, alog_ref, alogt_ref,
                  act_ref, dtp_ref, dtpt_ref, y_ref, state_ref):
    i = pl.program_id(1)
    n_steps = pl.num_programs(1)
    step = n_steps - 1 - i
    d_inner = n_heads * SSD_HEAD_DIM
    gn = N_GROUPS * D_STATE

    @pl.when(i == 0)
    def _():
        state_ref[...] = jnp.zeros_like(state_ref)

    dtp = jax.nn.softplus(dt_ref[...] + bias_ref[...])
    dtp_t = jax.nn.softplus(dtt_ref[...] + biast_ref[...])
    dtp_ref[...] = dtp
    dtpt_ref[...] = dtp_t

    ext_all = jnp.concatenate([jnp.where(step > 0, prev_ref[...], 0.0), xbc_ref[...],
                               jnp.where(step < n_steps - 1, next_ref[...], 0.0)], axis=0)
    ext_all = ext_all.astype(MXU_DTYPE)
    conv_w = convw_ref[...]
    tables = _decay_tables(dtp[:, n_heads:], dtp_t[n_heads:, :], alog_ref[:, n_heads:], alogt_ref[n_heads:, :],
                           reverse=True)
    for s in reversed(range(xbc_ref.shape[0] // CHUNK)):
        rows = slice(s * CHUNK, (s + 1) * CHUNK)
        ext = ext_all[s * CHUNK:(s + 1) * CHUNK + 2 * HALO]
        taps = jnp.concatenate([ext * conv_w[k:k + 1, :] for k in range(D_CONV)] + [convb_ref[...]], axis=0)
        half = jnp.dot(sel_ref[...], taps, preferred_element_type=F32)
        act = half + half * jnp.tanh(half)
        act_ref[rows, :] = act
        _scan_chunk(act[:, :d_inner], act[:, d_inner:d_inner + gn], act[:, d_inner + gn:],
                    tables, s * n_heads, n_heads, state_ref, y_ref.at[rows], reverse=True)


def _conv_select():
    ext = CHUNK + 2 * HALO
    t = np.arange(CHUNK)[:, None]
    j = np.arange(D_CONV * ext + BIAS_ROWS)[None, :]
    taps = (j < D_CONV * ext) & ((j % ext) == t + HALO - CONV_PAD + j // ext)
    bias = (j >= D_CONV * ext) & (j < D_CONV * ext + 2)
    return jnp.asarray(taps | bias, dtype=MXU_DTYPE)


def _conv_bias_rows(conv_b):
    half = 0.5 * conv_b
    hi = half.astype(MXU_DTYPE)
    lo = (half - hi.astype(F32)).astype(MXU_DTYPE)
    return jnp.concatenate([hi[None], lo[None], jnp.zeros((BIAS_ROWS - 2, conv_b.shape[0]), MXU_DTYPE)], axis=0)


def _ssd_bwd(xbc, dt_raw, dt_raw_t, conv_w, conv_b, dt_bias, dt_bias_t, alog, alog_t, batch, n_heads):
    m, c = xbc.shape
    seq = m // batch
    assert seq % SSD_STEP == 0
    nc = seq // SSD_STEP
    hb = SSD_STEP // HALO
    d_inner = n_heads * SSD_HEAD_DIM
    sel = _conv_select()
    rev = lambda b, i: b * nc + (nc - 1 - i)
    row = lambda n: pl.BlockSpec((SSD_STEP, n), lambda b, i: (rev(b, i), 0))
    col = lambda n: pl.BlockSpec((n, SSD_STEP), lambda b, i: (0, rev(b, i)))
    prev = pl.BlockSpec((HALO, c), lambda b, i: (jnp.maximum(rev(b, i) * hb - 1, 0), 0))
    nxt = pl.BlockSpec((HALO, c), lambda b, i: (jnp.minimum((rev(b, i) + 1) * hb, m // HALO - 1), 0))
    return pl.pallas_call(
        functools.partial(_ssd_bwd_body, n_heads),
        grid=(batch, nc),
        in_specs=[row(c), prev, nxt, row(2 * n_heads), col(2 * n_heads),
                  _const_spec(sel.shape), _const_spec(conv_w.shape), _const_spec(conv_b.shape),
                  _const_spec(dt_bias.shape), _const_spec(dt_bias_t.shape),
                  _const_spec(alog.shape), _const_spec(alog_t.shape)],
        out_specs=[row(c), row(2 * n_heads), col(2 * n_heads), row(d_inner)],
        out_shape=[jax.ShapeDtypeStruct((m, c), F32),
                   jax.ShapeDtypeStruct((m, 2 * n_heads), F32),
                   jax.ShapeDtypeStruct((2 * n_heads, m), F32),
                   jax.ShapeDtypeStruct((m, d_inner), F32)],
        scratch_shapes=[pltpu.VMEM((N_GROUPS, D_STATE, d_inner // N_GROUPS), F32)],
        compiler_params=_params(2),
        name="ssd_bwd",
    )(xbc, xbc, xbc, dt_raw, dt_raw_t, sel, conv_w, conv_b, dt_bias, dt_bias_t, alog, alog_t)


def _ssd_fwd_body(n_heads, act_ref, dtp_ref, dtpt_ref, ybwd_ref, z_ref, x_ref, alog_ref, alogt_ref,
                  dskip_ref, normg_ref, wout_ref, gpost_ref, o_ref, y_ref, state_ref):
    d_inner = n_heads * SSD_HEAD_DIM
    gn = N_GROUPS * D_STATE
    group_width = d_inner // N_GROUPS

    @pl.when(pl.program_id(1) == 0)
    def _():
        state_ref[...] = jnp.zeros_like(state_ref)

    tables = _decay_tables(dtp_ref[:, :n_heads], dtpt_ref[:n_heads, :], alog_ref[:, :n_heads],
                           alogt_ref[:n_heads, :], reverse=False)
    for s in range(act_ref.shape[0] // CHUNK):
        rows = slice(s * CHUNK, (s + 1) * CHUNK)
        _scan_chunk(act_ref[rows, :d_inner], act_ref[rows, d_inner:d_inner + gn], act_ref[rows, d_inner + gn:],
                    tables, s * n_heads, n_heads, state_ref, y_ref.at[rows], reverse=False)

    zh = z_ref[...]
    y = (y_ref[...] + ybwd_ref[...] + act_ref[:, :d_inner] * dskip_ref[...]) * (zh + zh * jnp.tanh(zh))
    for g in range(N_GROUPS):
        lanes = slice(g * group_width, (g + 1) * group_width)
        yg = y[:, lanes]
        yg = yg * lax.rsqrt(jnp.mean(yg * yg, axis=-1, keepdims=True) + EPS)
        y_ref[:, lanes] = yg * normg_ref[:, lanes]
    mix = jnp.dot(y_ref[...].astype(MXU_DTYPE), wout_ref[...], preferred_element_type=F32)
    o_ref[...] = x_ref[...] + _rmsnorm(mix, gpost_ref[...])


def _ssd_fwd(act, dtp, dtp_t, y_bwd, z, x, alog, alog_t, d_skip, norm_g, w_out, g_post, batch, n_heads):
    m, c = act.shape
    d = x.shape[1]
    seq = m // batch
    assert seq % SSD_STEP == 0
    nc = seq // SSD_STEP
    d_inner = n_heads * SSD_HEAD_DIM
    row = lambda n: pl.BlockSpec((SSD_STEP, n), lambda b, i: (b * nc + i, 0))
    col = lambda n: pl.BlockSpec((n, SSD_STEP), lambda b, i: (0, b * nc + i))
    return pl.pallas_call(
        functools.partial(_ssd_fwd_body, n_heads),
        grid=(batch, nc),
        in_specs=[row(c), row(2 * n_heads), col(2 * n_heads), row(d_inner), row(d_inner), row(d),
                  _const_spec(alog.shape), _const_spec(alog_t.shape), _const_spec(d_skip.shape),
                  _const_spec(norm_g.shape), _const_spec(w_out.shape), _const_spec(g_post.shape)],
        out_specs=row(d),
        out_shape=jax.ShapeDtypeStruct((m, d), F32),
        scratch_shapes=[pltpu.VMEM((SSD_STEP, d_inner), F32),
                        pltpu.VMEM((N_GROUPS, D_STATE, d_inner // N_GROUPS), F32)],
        compiler_params=_params(2),
        name="ssd_fwd",
    )(act, dtp, dtp_t, y_bwd, z, x, alog, alog_t, d_skip, norm_g, w_out, g_post)


def _t5_bias_table(relb_ref, bias_ref, group):
    kj = lax.broadcasted_iota(jnp.int32, (3 * BLOCK, BLOCK), 0)
    qi = lax.broadcasted_iota(jnp.int32, (3 * BLOCK, BLOCK), 1)
    rel = kj - BLOCK - qi
    n = jnp.abs(rel)
    half = N_BUCKETS // 2
    max_exact = half // 2
    large = jnp.full_like(n, max_exact)
    for k in range(1, half - max_exact):
        large = large + (n * n >= (max_exact * max_exact) << k).astype(jnp.int32)
    bucket = jnp.where(rel > 0, half, 0) + jnp.where(n < max_exact, n, large)
    in_window = n <= WINDOW
    n_kv = bias_ref.shape[0] // N_EDGE_VARIANTS
    for kv in range(n_kv):
        for r in range(group):
            acc = jnp.zeros((3 * BLOCK, BLOCK), F32)
            for b in range(N_BUCKETS):
                acc = jnp.where(bucket == b, relb_ref[b, kv * group + r] * LOG2_E, acc)
            acc = jnp.where(in_window, acc, NEG_INF)
            for variant in range(N_EDGE_VARIANTS):
                tbl = acc
                if variant & 1:
                    tbl = jnp.where(kj >= BLOCK, tbl, NEG_INF)
                if variant & 2:
                    tbl = jnp.where(kj < 2 * BLOCK, tbl, NEG_INF)
                bias_ref[variant * n_kv + kv, :, r * BLOCK:(r + 1) * BLOCK] = tbl


def _attn_body(qt_ref, kp_ref, kc_ref, kn_ref, vtp_ref, vtc_ref, vtn_ref, x_ref, sink_ref, relb_ref,
               wout_ref, gpost_ref, o_ref, bias_ref, ctx_ref):
    step = pl.program_id(1)
    blocks_per_step = qt_ref.shape[1] // BLOCK
    last_block = pl.num_programs(1) * blocks_per_step - 1
    n_kv = bias_ref.shape[0] // N_EDGE_VARIANTS
    group = qt_ref.shape[0] // HEAD_DIM // n_kv
    width = group * BLOCK

    @pl.when((pl.program_id(0) == 0) & (step == 0))
    def _():
        _t5_bias_table(relb_ref, bias_ref, group)

    kb_all = jnp.concatenate([kp_ref[...], kc_ref[...], kn_ref[...]], axis=0)
    vtb_all = jnp.concatenate([vtp_ref[...], vtc_ref[...], vtn_ref[...]], axis=1)
    heads_per_lane_tile = 128 // HEAD_DIM
    for sub in range(blocks_per_step):
        blk = step * blocks_per_step + sub
        variant = (blk == 0).astype(jnp.int32) + 2 * (blk == last_block).astype(jnp.int32)
        kb = kb_all[sub * BLOCK:(sub + 3) * BLOCK]
        vtb = vtb_all[:, sub * BLOCK:(sub + 3) * BLOCK]
        q_lanes = slice(sub * BLOCK, (sub + 1) * BLOCK)
        for kv in range(n_kv):
            tile = kv // heads_per_lane_tile
            k_tile = kb[:, tile * 128:(tile + 1) * 128]
            q_g = jnp.concatenate([qt_ref[(kv * group + r) * HEAD_DIM:(kv * group + r + 1) * HEAD_DIM, q_lanes]
                                   for r in range(group)], axis=1)
            q_pad = jnp.concatenate([q_g if p == kv % heads_per_lane_tile else jnp.zeros_like(q_g)
                                     for p in range(heads_per_lane_tile)], axis=0)
            scores = jnp.dot(k_tile, q_pad, preferred_element_type=F32)
            scores = scores + bias_ref[variant * n_kv + kv]
            sink = sink_ref[:, kv * width:(kv + 1) * width]
            mx = jnp.maximum(jnp.max(scores, axis=0, keepdims=True), sink)
            e = jnp.exp2(scores - mx)
            denom = jnp.sum(e, axis=0, keepdims=True) + jnp.exp2(sink - mx)
            vt_h = vtb[kv * HEAD_DIM:(kv + 1) * HEAD_DIM, :]
            ctx_t = jnp.dot(vt_h, e.astype(MXU_DTYPE), preferred_element_type=F32) / denom
            for r in range(group):
                h = kv * group + r
                ctx_ref[h * HEAD_DIM:(h + 1) * HEAD_DIM, q_lanes] = ctx_t[:, r * BLOCK:(r + 1) * BLOCK]
    ctx = ctx_ref[...].T
    mix = jnp.dot(ctx.astype(MXU_DTYPE), wout_ref[...], preferred_element_type=F32)
    o_ref[...] = x_ref[...] + _rmsnorm(mix, gpost_ref[...])


def _attn(q_t, k, v_t, x, sink_row, rel_bias, w_out, g_post, batch):
    qd, m = q_t.shape
    kvd = k.shape[1]
    d = x.shape[1]
    n_kv = kvd // HEAD_DIM
    group = qd // kvd
    assert (m // batch) % ATTN_STEP == 0
    nb = m // batch // BLOCK
    bps = ATTN_STEP // BLOCK
    ns = nb // bps
    cur_i = lambda b, i: b * ns + i
    prev_i = lambda b, i: b * nb + jnp.maximum(i * bps - 1, 0)
    next_i = lambda b, i: b * nb + jnp.minimum((i + 1) * bps, nb - 1)
    rows = lambda size, n, f: pl.BlockSpec((size, n), lambda b, i: (f(b, i), 0))
    cols = lambda size, n, f: pl.BlockSpec((n, size), lambda b, i: (0, f(b, i)))
    smem = pl.BlockSpec(memory_space=pltpu.SMEM)
    return pl.pallas_call(
        _attn_body,
        grid=(batch, ns),
        in_specs=[cols(ATTN_STEP, qd, cur_i),
                  rows(BLOCK, kvd, prev_i), rows(ATTN_STEP, kvd, cur_i), rows(BLOCK, kvd, next_i),
                  cols(BLOCK, kvd, prev_i), cols(ATTN_STEP, kvd, cur_i), cols(BLOCK, kvd, next_i),
                  rows(ATTN_STEP, d, cur_i),
                  _const_spec(sink_row.shape), smem, _const_spec(w_out.shape), _const_spec(g_post.shape)],
        out_specs=rows(ATTN_STEP, d, cur_i),
        out_shape=jax.ShapeDtypeStruct((m, d), F32),
        scratch_shapes=[pltpu.VMEM((N_EDGE_VARIANTS * n_kv, 3 * BLOCK, group * BLOCK), F32),
                        pltpu.VMEM((qd, ATTN_STEP), F32)],
        compiler_params=_params(2),
        name="attn",
    )(q_t, k, k, k, v_t, v_t, v_t, x, sink_row, rel_bias, w_out, g_post)


def _row(v):
    return v.reshape(1, -1)


def _lane_bcast(r, width):
    return jnp.broadcast_to(r.reshape(-1, 1), (r.shape[1], width))


def _prep_ssd(w_in, conv_w, conv_b, dt_bias, a_log, d_skip, norm_g, w_out):
    n_heads = d_skip.shape[0]
    d_inner = n_heads * SSD_HEAD_DIM
    conv_dim = conv_w.shape[1]
    w_dt = w_in[:, d_inner + conv_dim:].astype(MXU_DTYPE)
    bias = dt_bias.reshape(1, -1)
    alog = a_log.reshape(1, -1)
    return dict(n_heads=n_heads, w_z=(0.5 * w_in[:, :d_inner]).astype(MXU_DTYPE),
                w_xbc=w_in[:, d_inner:d_inner + conv_dim].astype(MXU_DTYPE), w_dt=w_dt, w_dt_t=w_dt.T,
                conv_w=(0.5 * conv_w).astype(MXU_DTYPE), conv_b=_conv_bias_rows(conv_b),
                bias=bias, bias_t=_lane_bcast(bias, SSD_STEP),
                alog=alog, alog_t=_lane_bcast(alog, SSD_STEP), d_skip=_row(jnp.repeat(d_skip, SSD_HEAD_DIM)),
                norm_g=_row(norm_g), w_out=w_out.astype(MXU_DTYPE))


def _ssd_layer(x, batch, g_pre, g_post, p):
    z, xbc, dt_raw, dt_raw_t = _norm_proj(x, g_pre, [p["w_z"], p["w_xbc"], p["w_dt"]], [p["w_dt_t"]])
    act, dtp, dtp_t, y_bwd = _ssd_bwd(xbc, dt_raw, dt_raw_t, p["conv_w"], p["conv_b"], p["bias"], p["bias_t"],
                                      p["alog"], p["alog_t"], batch, p["n_heads"])
    return _ssd_fwd(act, dtp, dtp_t, y_bwd, z, x, p["alog"], p["alog_t"], p["d_skip"], p["norm_g"],
                    p["w_out"], g_post, batch, p["n_heads"])


def _prep_attn(w_qkv, sink, w_out):
    q_dim = w_out.shape[0]
    kv_dim = (w_qkv.shape[1] - q_dim) // 2
    w_q_t = (w_qkv[:, :q_dim] * (LOG2_E * HEAD_DIM ** -0.5)).T.astype(MXU_DTYPE)
    return dict(w_q_t=w_q_t, w_k=w_qkv[:, q_dim:q_dim + kv_dim].astype(MXU_DTYPE),
                w_v_t=w_qkv[:, q_dim + kv_dim:].T.astype(MXU_DTYPE),
                sink_row=_row(jnp.repeat(sink * LOG2_E, BLOCK)), w_out=w_out.astype(MXU_DTYPE))


def _attn_layer(x, batch, g_pre, g_post, p, rel_bias):
    k, q_t, v_t = _norm_proj(x, g_pre, [p["w_k"]], [p["w_q_t"], p["w_v_t"]], dtypes=[MXU_DTYPE] * 3)
    return _attn(q_t, k, v_t, x, p["sink_row"], rel_bias, p["w_out"], g_post, batch)


def _encoder(x3, norm_g, ffn, mixers, rel_bias):
    batch, seq, d = x3.shape
    x = x3.reshape(batch * seq, d)
    for i, mixer in enumerate(mixers):
        ng = [_row(norm_g[i, s]) for s in range(norm_g.shape[1])]
        x = _ffn(x, ng[0], ng[1], *ffn[i][0])
        if "w_z" in mixer:
            x = _ssd_layer(x, batch, ng[2], ng[3], mixer)
        else:
            x = _attn_layer(x, batch, ng[2], ng[3], mixer, rel_bias)
        x = _ffn(x, ng[4], ng[5], *ffn[i][1])
    return x.reshape(batch, seq, d)


def kernel(x_prompt, x_sample, norm_g, ffn_w_gate, ffn_w_up, ffn_w_down, ssd_w_in, ssd_conv_w, ssd_conv_b,
           ssd_dt_bias, ssd_A_log, ssd_D, ssd_norm_g, ssd_w_out, attn_w_qkv, attn_sink, attn_w_out, rel_bias):
    depth = norm_g.shape[0]
    ffn = [[(ffn_w_gate[i, s].astype(MXU_DTYPE), ffn_w_up[i, s].astype(MXU_DTYPE),
             ffn_w_down[i, s].astype(MXU_DTYPE)) for s in range(2)] for i in range(depth)]
    mixers = []
    for i in range(depth):
        j = i // 2
        if i % 2 == 0:
            mixers.append(_prep_ssd(ssd_w_in[j], ssd_conv_w[j], ssd_conv_b[j], ssd_dt_bias[j], ssd_A_log[j],
                                    ssd_D[j], ssd_norm_g[j], ssd_w_out[j]))
        else:
            mixers.append(_prep_attn(attn_w_qkv[j], attn_sink[j], attn_w_out[j]))
    return tuple(_encoder(x, norm_g, ffn, mixers, rel_bias) for x in (x_prompt, x_sample))
```

```python
import functools

import jax
import jax.numpy as jnp
import numpy as np
from jax import lax
from jax.experimental import pallas as pl
from jax.experimental.pallas import tpu as pltpu

F32 = jnp.float32
MXU_DTYPE = jnp.bfloat16

EPS = 1e-6
SSD_HEAD_DIM = 64
N_GROUPS = 8
D_STATE = 128
CHUNK = 128
D_CONV = 5
CONV_PAD = D_CONV // 2
HALO = 8
BIAS_ROWS = 16
SSD_STEP = 4 * CHUNK
HEAD_DIM = 64
N_KV_HEADS = 4
BLOCK = 128
WINDOW = 128
N_BUCKETS = 32
FFN_TILE = 1024
FFN_SUBTILE = 256
PROJ_TILE = 512
ATTN_STEP = 8 * BLOCK
N_EDGE_VARIANTS = 4

V7X_VMEM_BYTES = 64 * 1024 * 1024
VMEM_LIMIT_BYTES = V7X_VMEM_BYTES - 8 * 1024 * 1024

NEG_INF = float("-inf")
LOG2_E = 1.4426950408889634


def _rmsnorm(x, g):
    return x * lax.rsqrt(jnp.mean(x * x, axis=-1, keepdims=True) + EPS) * g


def _silu(x):
    t = 0.5 * x
    return t + t * jnp.tanh(t)


def _mm(a, b):
    return jnp.dot(a.astype(MXU_DTYPE), b.astype(MXU_DTYPE), preferred_element_type=F32)


def _mm_nt(a, b):
    return lax.dot_general(a.astype(MXU_DTYPE), b.astype(MXU_DTYPE), (((1,), (1,)), ((), ())),
                           preferred_element_type=F32)


def _mm_f32(a, b):
    return jnp.dot(a, b, precision=lax.Precision.HIGHEST, preferred_element_type=F32)


def _const_spec(shape):
    zeros = (0,) * len(shape)
    return pl.BlockSpec(shape, lambda *_: zeros, pipeline_mode=pl.Buffered(1))


def _params(n_axes):
    return pltpu.CompilerParams(dimension_semantics=("arbitrary",) * n_axes,
                                vmem_limit_bytes=VMEM_LIMIT_BYTES)


def _ffn_body(x_ref, gpre_ref, gpost_ref, wg_ref, wu_ref, wd_ref, o_ref):
    for s in range(x_ref.shape[0] // FFN_SUBTILE):
        rows = slice(s * FFN_SUBTILE, (s + 1) * FFN_SUBTILE)
        x = x_ref[rows, :]
        h = _rmsnorm(x, gpre_ref[...]).astype(MXU_DTYPE)
        g = jnp.dot(h, wg_ref[...], preferred_element_type=F32)
        u = jnp.dot(h, wu_ref[...], preferred_element_type=F32)
        a = (_silu(g) * u).astype(MXU_DTYPE)
        y = jnp.dot(a, wd_ref[...], preferred_element_type=F32)
        o_ref[rows, :] = x + 0.5 * _rmsnorm(y, gpost_ref[...])


def _ffn(x, g_pre, g_post, wg, wu, wd):
    m, d = x.shape
    f = wg.shape[1]
    tm = FFN_TILE
    assert m % tm == 0 and tm % FFN_SUBTILE == 0
    row = pl.BlockSpec((tm, d), lambda i: (i, 0))
    return pl.pallas_call(
        _ffn_body,
        grid=(m // tm,),
        in_specs=[row, _const_spec((1, d)), _const_spec((1, d)),
                  _const_spec((d, f)), _const_spec((d, f)), _const_spec((f, d))],
        out_specs=row,
        out_shape=jax.ShapeDtypeStruct((m, d), x.dtype),
        compiler_params=_params(1),
        name="ffn",
    )(x, g_pre, g_post, wg, wu, wd)


def _norm_proj_body(n_w, n_wt, x_ref, g_ref, *refs):
    w_refs = refs[:n_w]
    wt_refs = refs[n_w:n_w + n_wt]
    o_refs = refs[n_w + n_wt:]
    h = _rmsnorm(x_ref[...], g_ref[...]).astype(MXU_DTYPE)
    for w_ref, o_ref in zip(w_refs, o_refs[:n_w]):
        o_ref[...] = jnp.dot(h, w_ref[...], preferred_element_type=F32).astype(o_ref.dtype)
    for wt_ref, o_ref in zip(wt_refs, o_refs[n_w:]):
        o_ref[...] = _mm_nt(wt_ref[...], h).astype(o_ref.dtype)


def _norm_proj(x, g, ws, wts, dtypes=None):
    m, d = x.shape
    tm = PROJ_TILE
    assert m % tm == 0
    dtypes = dtypes or [F32] * (len(ws) + len(wts))
    row = lambda n: pl.BlockSpec((tm, n), lambda i: (i, 0))
    col = lambda n: pl.BlockSpec((n, tm), lambda i: (0, i))
    return pl.pallas_call(
        functools.partial(_norm_proj_body, len(ws), len(wts)),
        grid=(m // tm,),
        in_specs=[row(d), _const_spec((1, d))] + [_const_spec(w.shape) for w in ws]
                 + [_const_spec(w.shape) for w in wts],
        out_specs=[row(w.shape[1]) for w in ws] + [col(w.shape[0]) for w in wts],
        out_shape=[jax.ShapeDtypeStruct((m, w.shape[1]), t) for w, t in zip(ws, dtypes)]
                  + [jax.ShapeDtypeStruct((w.shape[0], m), t) for w, t in zip(wts, dtypes[len(ws):])],
        compiler_params=_params(1),
        name="norm_proj",
    )(x, g, *ws, *wts)


def _decay_tables(dt, dt_t, alog, alog_t, reverse):
    n_chunks = dt.shape[0] // CHUNK
    row = lax.broadcasted_iota(jnp.int32, (CHUNK, CHUNK), 0)
    col = lax.broadcasted_iota(jnp.int32, (CHUNK, CHUNK), 1)
    valid = (col >= row) if reverse else (col <= row)
    valid_t = (col <= row) if reverse else (col >= row)
    a = dt * (-LOG2_E * jnp.exp(alog))
    a_t = dt_t * (-LOG2_E * jnp.exp(alog_t))
    chunks = [slice(c * CHUNK, (c + 1) * CHUNK) for c in range(n_chunks)]
    a_cols = jnp.concatenate([a[c] for c in chunks], axis=1)
    a_rows = jnp.concatenate([a_t[:, c] for c in chunks], axis=0)
    dt_rows = jnp.concatenate([dt_t[:, c] for c in chunks], axis=0)
    acs = _mm_f32(valid.astype(F32), a_cols)
    acs_t = _mm_f32(a_rows, valid_t.astype(F32))
    tot_t = _mm_f32(a_rows, jnp.ones((CHUNK, CHUNK), F32))
    w_t = dt_rows * jnp.exp2(tot_t - acs_t)
    src_t = acs_t - jnp.log2(dt_rows)
    return valid, acs, src_t, w_t


def _scan_chunk(xs, bm, cm, tables, base, n_heads, state_ref, y_ref, reverse):
    valid, acs, src_t, w_t = tables
    length = xs.shape[0]
    heads_per_group = n_heads // N_GROUPS
    group_width = heads_per_group * SSD_HEAD_DIM
    assert group_width % 128 == 0 and 128 % SSD_HEAD_DIM == 0
    heads_per_tile = 128 // SSD_HEAD_DIM
    end = 0 if reverse else length - 1

    stack_row = lax.broadcasted_iota(jnp.int32, (heads_per_group * length, group_width), 0)
    stack_col = lax.broadcasted_iota(jnp.int32, (heads_per_group * length, group_width), 1)
    block_diag = (stack_row // length) == (stack_col // SSD_HEAD_DIM)
    head_in_tile = lax.broadcasted_iota(jnp.int32, (length, 128), 1) // SSD_HEAD_DIM

    for g in range(N_GROUPS):
        bg = bm[:, g * D_STATE:(g + 1) * D_STATE]
        cg = cm[:, g * D_STATE:(g + 1) * D_STATE].astype(MXU_DTYPE)
        bg_t = bg.T
        prev = state_ref[g]
        c_both = _mm(cg, jnp.concatenate([bg_t.astype(MXU_DTYPE), prev.astype(MXU_DTYPE)], axis=1))
        cb = c_both[:, :length]
        lhs_y, lhs_s, acs_cols = [], [], []
        for r in range(heads_per_group):
            h = base + g * heads_per_group + r
            acs_col = jnp.broadcast_to(acs[:, h:h + 1], (length, length))
            seg_dt = jnp.exp2(jnp.where(valid, acs_col - src_t[h:h + 1, :], NEG_INF))
            lhs_y.append((cb * seg_dt).astype(MXU_DTYPE))
            lhs_s.append((bg_t * w_t[h:h + 1, :]).astype(MXU_DTYPE))
            acs_cols.append(acs_col)
        lanes = slice(g * group_width, (g + 1) * group_width)
        x_g = xs[:, lanes].astype(MXU_DTYPE)
        rhs = jnp.where(block_diag, jnp.concatenate([x_g] * heads_per_group, axis=0), jnp.zeros((), MXU_DTYPE))
        y_diag = jnp.dot(jnp.concatenate(lhs_y, axis=1), rhs, preferred_element_type=F32)
        new = jnp.dot(jnp.concatenate(lhs_s, axis=1), rhs, preferred_element_type=F32)
        tiles = []
        for t in range(group_width // 128):
            tile = acs_cols[t * heads_per_tile]
            for p in range(1, heads_per_tile):
                tile = jnp.where(head_in_tile == p, acs_cols[t * heads_per_tile + p], tile)
            tiles.append(tile)
        off = jnp.exp2(jnp.concatenate(tiles, axis=1))
        y_ref[:, lanes] = y_diag + c_both[:, length:] * off
        state_ref[g] = prev * off[end:end + 1, :] + new


def _ssd_bwd_body(n_heads, xbc_ref, prev_ref, next_ref, dt_ref, dtt_ref, sel_ref, convw_ref, convb_ref,
                  bias_ref, biast_ref, alog_ref, alogt_ref,
                  act_ref, dtp_ref, dtpt_ref, y_ref, state_ref):
    i = pl.program_id(1)
    n_steps = pl.num_programs(1)
    step = n_steps - 1 - i
    d_inner = n_heads * SSD_HEAD_DIM
    gn = N_GROUPS * D_STATE

    @pl.when(i == 0)
    def _():
        state_ref[...] = jnp.zeros_like(state_ref)

    dtp = jax.nn.softplus(dt_ref[...] + bias_ref[...])
    dtp_t = jax.nn.softplus(dtt_ref[...] + biast_ref[...])
    dtp_ref[...] = dtp
    dtpt_ref[...] = dtp_t

    ext_all = jnp.concatenate([jnp.where(step > 0, prev_ref[...], 0.0), xbc_ref[...],
                               jnp.where(step < n_steps - 1, next_ref[...], 0.0)], axis=0)
    ext_all = ext_all.astype(MXU_DTYPE)
    conv_w = convw_ref[...]
    tables = _decay_tables(dtp[:, n_heads:], dtp_t[n_heads:, :], alog_ref[:, n_heads:], alogt_ref[n_heads:, :],
                           reverse=True)
    for s in reversed(range(xbc_ref.shape[0] // CHUNK)):
        rows = slice(s * CHUNK, (s + 1) * CHUNK)
        ext = ext_all[s * CHUNK:(s + 1) * CHUNK + 2 * HALO]
        taps = jnp.concatenate([ext * conv_w[k:k + 1, :] for k in range(D_CONV)] + [convb_ref[...]], axis=0)
        half = jnp.dot(sel_ref[...], taps, preferred_element_type=F32)
        act = half + half * jnp.tanh(half)
        act_ref[rows, :] = act
        _scan_chunk(act[:, :d_inner], act[:, d_inner:d_inner + gn], act[:, d_inner + gn:],
                    tables, s * n_heads, n_heads, state_ref, y_ref.at[rows], reverse=True)


def _conv_select():
    ext = CHUNK + 2 * HALO
    t = np.arange(CHUNK)[:, None]
    j = np.arange(D_CONV * ext + BIAS_ROWS)[None, :]
    taps = (j < D_CONV * ext) & ((j % ext) == t + HALO - CONV_PAD + j // ext)
    bias = (j >= D_CONV * ext) & (j < D_CONV * ext + 2)
    return jnp.asarray(taps | bias, dtype=MXU_DTYPE)


def _conv_bias_rows(conv_b):
    half = 0.5 * conv_b
    hi = half.astype(MXU_DTYPE)
    lo = (half - hi.astype(F32)).astype(MXU_DTYPE)
    return jnp.concatenate([hi[None], lo[None], jnp.zeros((BIAS_ROWS - 2, conv_b.shape[0]), MXU_DTYPE)], axis=0)


def _ssd_bwd(xbc, dt_raw, dt_raw_t, conv_w, conv_b, dt_bias, dt_bias_t, alog, alog_t, batch, n_heads):
    m, c = xbc.shape
    seq = m // batch
    assert seq % SSD_STEP == 0
    nc = seq // SSD_STEP
    hb = SSD_STEP // HALO
    d_inner = n_heads * SSD_HEAD_DIM
    sel = _conv_select()
    rev = lambda b, i: b * nc + (nc - 1 - i)
    row = lambda n: pl.BlockSpec((SSD_STEP, n), lambda b, i: (rev(b, i), 0))
    col = lambda n: pl.BlockSpec((n, SSD_STEP), lambda b, i: (0, rev(b, i)))
    prev = pl.BlockSpec((HALO, c), lambda b, i: (jnp.maximum(rev(b, i) * hb - 1, 0), 0))
    nxt = pl.BlockSpec((HALO, c), lambda b, i: (jnp.minimum((rev(b, i) + 1) * hb, m // HALO - 1), 0))
    return pl.pallas_call(
        functools.partial(_ssd_bwd_body, n_heads),
        grid=(batch, nc),
        in_specs=[row(c), prev, nxt, row(2 * n_heads), col(2 * n_heads),
                  _const_spec(sel.shape), _const_spec(conv_w.shape), _const_spec(conv_b.shape),
                  _const_spec(dt_bias.shape), _const_spec(dt_bias_t.shape),
                  _const_spec(alog.shape), _const_spec(alog_t.shape)],
        out_specs=[row(c), row(2 * n_heads), col(2 * n_heads), row(d_inner)],
        out_shape=[jax.ShapeDtypeStruct((m, c), F32),
                   jax.ShapeDtypeStruct((m, 2 * n_heads), F32),
                   jax.ShapeDtypeStruct((2 * n_heads, m), F32),
                   jax.ShapeDtypeStruct((m, d_inner), F32)],
        scratch_shapes=[pltpu.VMEM((N_GROUPS, D_STATE, d_inner // N_GROUPS), F32)],
        compiler_params=_params(2),
        name="ssd_bwd",
    )(xbc, xbc, xbc, dt_raw, dt_raw_t, sel, conv_w, conv_b, dt_bias, dt_bias_t, alog, alog_t)


def _ssd_fwd_body(n_heads, act_ref, dtp_ref, dtpt_ref, ybwd_ref, z_ref, x_ref, alog_ref, alogt_ref,
                  dskip_ref, normg_ref, wout_ref, gpost_ref, o_ref, y_ref, state_ref):
    d_inner = n_heads * SSD_HEAD_DIM
    gn = N_GROUPS * D_STATE
    group_width = d_inner // N_GROUPS

    @pl.when(pl.program_id(1) == 0)
    def _():
        state_ref[...] = jnp.zeros_like(state_ref)

    tables = _decay_tables(dtp_ref[:, :n_heads], dtpt_ref[:n_heads, :], alog_ref[:, :n_heads],
                           alogt_ref[:n_heads, :], reverse=False)
    for s in range(act_ref.shape[0] // CHUNK):
        rows = slice(s * CHUNK, (s + 1) * CHUNK)
        _scan_chunk(act_ref[rows, :d_inner], act_ref[rows, d_inner:d_inner + gn], act_ref[rows, d_inner + gn:],
                    tables, s * n_heads, n_heads, state_ref, y_ref.at[rows], reverse=False)

    zh = z_ref[...]
    y = (y_ref[...] + ybwd_ref[...] + act_ref[:, :d_inner] * dskip_ref[...]) * (zh + zh * jnp.tanh(zh))
    for g in range(N_GROUPS):
        lanes = slice(g * group_width, (g + 1) * group_width)
        yg = y[:, lanes]
        yg = yg * lax.rsqrt(jnp.mean(yg * yg, axis=-1, keepdims=True) + EPS)
        y_ref[:, lanes] = yg * normg_ref[:, lanes]
    mix = jnp.dot(y_ref[...].astype(MXU_DTYPE), wout_ref[...], preferred_element_type=F32)
    o_ref[...] = x_ref[...] + _rmsnorm(mix, gpost_ref[...])


def _ssd_fwd(act, dtp, dtp_t, y_bwd, z, x, alog, alog_t, d_skip, norm_g, w_out, g_post, batch, n_heads):
    m, c = act.shape
    d = x.shape[1]
    seq = m // batch
    assert seq % SSD_STEP == 0
    nc = seq // SSD_STEP
    d_inner = n_heads * SSD_HEAD_DIM
    row = lambda n: pl.BlockSpec((SSD_STEP, n), lambda b, i: (b * nc + i, 0))
    col = lambda n: pl.BlockSpec((n, SSD_STEP), lambda b, i: (0, b * nc + i))
    return pl.pallas_call(
        functools.partial(_ssd_fwd_body, n_heads),
        grid=(batch, nc),
        in_specs=[row(c), row(2 * n_heads), col(2 * n_heads), row(d_inner), row(d_inner), row(d),
                  _const_spec(alog.shape), _const_spec(alog_t.shape), _const_spec(d_skip.shape),
                  _const_spec(norm_g.shape), _const_spec(w_out.shape), _const_spec(g_post.shape)],
        out_specs=row(d),
        out_shape=jax.ShapeDtypeStruct((m, d), F32),
        scratch_shapes=[pltpu.VMEM((SSD_STEP, d_inner), F32),
                        pltpu.VMEM((N_GROUPS, D_STATE, d_inner // N_GROUPS), F32)],
        compiler_params=_params(2),
        name="ssd_fwd",
    )(act, dtp, dtp_t, y_bwd, z, x, alog, alog_t, d_skip, norm_g, w_out, g_post)


def _t5_bias_table(relb_ref, bias_ref, group):
    kj = lax.broadcasted_iota(jnp.int32, (3 * BLOCK, BLOCK), 0)
    qi = lax.broadcasted_iota(jnp.int32, (3 * BLOCK, BLOCK), 1)
    rel = kj - BLOCK - qi
    n = jnp.abs(rel)
    half = N_BUCKETS // 2
    max_exact = half // 2
    large = jnp.full_like(n, max_exact)
    for k in range(1, half - max_exact):
        large = large + (n * n >= (max_exact * max_exact) << k).astype(jnp.int32)
    bucket = jnp.where(rel > 0, half, 0) + jnp.where(n < max_exact, n, large)
    in_window = n <= WINDOW
    n_kv = bias_ref.shape[0] // N_EDGE_VARIANTS
    for kv in range(n_kv):
        for r in range(group):
            acc = jnp.zeros((3 * BLOCK, BLOCK), F32)
            for b in range(N_BUCKETS):
                acc = jnp.where(bucket == b, relb_ref[b, kv * group + r] * LOG2_E, acc)
            acc = jnp.where(in_window, acc, NEG_INF)
            for variant in range(N_EDGE_VARIANTS):
                tbl = acc
                if variant & 1:
                    tbl = jnp.where(kj >= BLOCK, tbl, NEG_INF)
                if variant & 2:
                    tbl = jnp.where(kj < 2 * BLOCK, tbl, NEG_INF)
                bias_ref[variant * n_kv + kv, :, r * BLOCK:(r + 1) * BLOCK] = tbl


def _attn_body(qt_ref, kp_ref, kc_ref, kn_ref, vtp_ref, vtc_ref, vtn_ref, x_ref, sink_ref, relb_ref,
               wout_ref, gpost_ref, o_ref, bias_ref, ctx_ref):
    step = pl.program_id(1)
    blocks_per_step = qt_ref.shape[1] // BLOCK
    last_block = pl.num_programs(1) * blocks_per_step - 1
    n_kv = bias_ref.shape[0] // N_EDGE_VARIANTS
    group = qt_ref.shape[0] // HEAD_DIM // n_kv
    width = group * BLOCK

    @pl.when((pl.program_id(0) == 0) & (step == 0))
    def _():
        _t5_bias_table(relb_ref, bias_ref, group)

    kb_all = jnp.concatenate([kp_ref[...], kc_ref[...], kn_ref[...]], axis=0)
    vtb_all = jnp.concatenate([vtp_ref[...], vtc_ref[...], vtn_ref[...]], axis=1)
    for sub in range(blocks_per_step):
        blk = step * blocks_per_step + sub
        variant = (blk == 0).astype(jnp.int32) + 2 * (blk == last_block).astype(jnp.int32)
        kb = kb_all[sub * BLOCK:(sub + 3) * BLOCK]
        vtb = vtb_all[:, sub * BLOCK:(sub + 3) * BLOCK]
        q_lanes = slice(sub * BLOCK, (sub + 1) * BLOCK)
        q_cols = []
        for kv in range(n_kv):
            q_g = jnp.concatenate([qt_ref[(kv * group + r) * HEAD_DIM:(kv * group + r + 1) * HEAD_DIM, q_lanes]
                                   for r in range(group)], axis=1)
            q_cols.append(jnp.concatenate([q_g if p == kv else jnp.zeros_like(q_g) for p in range(n_kv)],
                                          axis=0))
        scores_all = jnp.dot(kb, jnp.concatenate(q_cols, axis=1), preferred_element_type=F32)
        for kv in range(n_kv):
            scores = scores_all[:, kv * width:(kv + 1) * width] + bias_ref[variant * n_kv + kv]
            sink = sink_ref[:, kv * width:(kv + 1) * width]
            mx = jnp.maximum(jnp.max(scores, axis=0, keepdims=True), sink)
            e = jnp.exp2(scores - mx)
            denom = jnp.sum(e, axis=0, keepdims=True) + jnp.exp2(sink - mx)
            vt_h = vtb[kv * HEAD_DIM:(kv + 1) * HEAD_DIM, :]
            ctx_t = jnp.dot(vt_h, e.astype(MXU_DTYPE), preferred_element_type=F32) / denom
            for r in range(group):
                h = kv * group + r
                ctx_ref[h * HEAD_DIM:(h + 1) * HEAD_DIM, q_lanes] = ctx_t[:, r * BLOCK:(r + 1) * BLOCK]
    ctx = ctx_ref[...].T
    mix = jnp.dot(ctx.astype(MXU_DTYPE), wout_ref[...], preferred_element_type=F32)
    o_ref[...] = x_ref[...] + _rmsnorm(mix, gpost_ref[...])


def _attn(q_t, k, v_t, x, sink_row, rel_bias, w_out, g_post, batch):
    qd, m = q_t.shape
    kvd = k.shape[1]
    d = x.shape[1]
    n_kv = kvd // HEAD_DIM
    group = qd // kvd
    assert (m // batch) % ATTN_STEP == 0
    nb = m // batch // BLOCK
    bps = ATTN_STEP // BLOCK
    ns = nb // bps
    cur_i = lambda b, i: b * ns + i
    prev_i = lambda b, i: b * nb + jnp.maximum(i * bps - 1, 0)
    next_i = lambda b, i: b * nb + jnp.minimum((i + 1) * bps, nb - 1)
    rows = lambda size, n, f: pl.BlockSpec((size, n), lambda b, i: (f(b, i), 0))
    cols = lambda size, n, f: pl.BlockSpec((n, size), lambda b, i: (0, f(b, i)))
    smem = pl.BlockSpec(memory_space=pltpu.SMEM)
    return pl.pallas_call(
        _attn_body,
        grid=(batch, ns),
        in_specs=[cols(ATTN_STEP, qd, cur_i),
                  rows(BLOCK, kvd, prev_i), rows(ATTN_STEP, kvd, cur_i), rows(BLOCK, kvd, next_i),
                  cols(BLOCK, kvd, prev_i), cols(ATTN_STEP, kvd, cur_i), cols(BLOCK, kvd, next_i),
                  rows(ATTN_STEP, d, cur_i),
                  _const_spec(sink_row.shape), smem, _const_spec(w_out.shape), _const_spec(g_post.shape)],
        out_specs=rows(ATTN_STEP, d, cur_i),
        out_shape=jax.ShapeDtypeStruct((m, d), F32),
        scratch_shapes=[pltpu.VMEM((N_EDGE_VARIANTS * n_kv, 3 * BLOCK, group * BLOCK), F32),
                        pltpu.VMEM((qd, ATTN_STEP), F32)],
        compiler_params=_params(2),
        name="attn",
    )(q_t, k, k, k, v_t, v_t, v_t, x, sink_row, rel_bias, w_out, g_post)


def _row(v):
    return v.reshape(1, -1)


def _lane_bcast(r, width):
    return jnp.broadcast_to(r.reshape(-1, 1), (r.shape[1], width))


def _prep_ssd(w_in, conv_w, conv_b, dt_bias, a_log, d_skip, norm_g, w_out):
    n_heads = d_skip.shape[0]
    d_inner = n_heads * SSD_HEAD_DIM
    conv_dim = conv_w.shape[1]
    w_dt = w_in[:, d_inner + conv_dim:].astype(MXU_DTYPE)
    bias = dt_bias.reshape(1, -1)
    alog = a_log.reshape(1, -1)
    return dict(n_heads=n_heads, w_z=(0.5 * w_in[:, :d_inner]).astype(MXU_DTYPE),
                w_xbc=w_in[:, d_inner:d_inner + conv_dim].astype(MXU_DTYPE), w_dt=w_dt, w_dt_t=w_dt.T,
                conv_w=(0.5 * conv_w).astype(MXU_DTYPE), conv_b=_conv_bias_rows(conv_b),
                bias=bias, bias_t=_lane_bcast(bias, SSD_STEP),
                alog=alog, alog_t=_lane_bcast(alog, SSD_STEP), d_skip=_row(jnp.repeat(d_skip, SSD_HEAD_DIM)),
                norm_g=_row(norm_g), w_out=w_out.astype(MXU_DTYPE))


def _ssd_layer(x, batch, g_pre, g_post, p):
    z, xbc, dt_raw, dt_raw_t = _norm_proj(x, g_pre, [p["w_z"], p["w_xbc"], p["w_dt"]], [p["w_dt_t"]])
    act, dtp, dtp_t, y_bwd = _ssd_bwd(xbc, dt_raw, dt_raw_t, p["conv_w"], p["conv_b"], p["bias"], p["bias_t"],
                                      p["alog"], p["alog_t"], batch, p["n_heads"])
    return _ssd_fwd(act, dtp, dtp_t, y_bwd, z, x, p["alog"], p["alog_t"], p["d_skip"], p["norm_g"],
                    p["w_out"], g_post, batch, p["n_heads"])


def _prep_attn(w_qkv, sink, w_out):
    q_dim = w_out.shape[0]
    kv_dim = (w_qkv.shape[1] - q_dim) // 2
    w_q_t = (w_qkv[:, :q_dim] * (LOG2_E * HEAD_DIM ** -0.5)).T.astype(MXU_DTYPE)
    return dict(w_q_t=w_q_t, w_k=w_qkv[:, q_dim:q_dim + kv_dim].astype(MXU_DTYPE),
                w_v_t=w_qkv[:, q_dim + kv_dim:].T.astype(MXU_DTYPE),
                sink_row=_row(jnp.repeat(sink * LOG2_E, BLOCK)), w_out=w_out.astype(MXU_DTYPE))


def _attn_layer(x, batch, g_pre, g_post, p, rel_bias):
    k, q_t, v_t = _norm_proj(x, g_pre, [p["w_k"]], [p["w_q_t"], p["w_v_t"]], dtypes=[MXU_DTYPE] * 3)
    return _attn(q_t, k, v_t, x, p["sink_row"], rel_bias, p["w_out"], g_post, batch)


def _encoder(x3, norm_g, ffn, mixers, rel_bias):
    batch, seq, d = x3.shape
    x = x3.reshape(batch * seq, d)
    for i, mixer in enumerate(mixers):
        ng = [_row(norm_g[i, s]) for s in range(norm_g.shape[1])]
        x = _ffn(x, ng[0], ng[1], *ffn[i][0])
        if "w_z" in mixer:
            x = _ssd_layer(x, batch, ng[2], ng[3], mixer)
        else:
            x = _attn_layer(x, batch, ng[2], ng[3], mixer, rel_bias)
        x = _ffn(x, ng[4], ng[5], *ffn[i][1])
    return x.reshape(batch, seq, d)


def kernel(x_prompt, x_sample, norm_g, ffn_w_gate, ffn_w_up, ffn_w_down, ssd_w_in, ssd_conv_w, ssd_conv_b,
           ssd_dt_bias, ssd_A_log, ssd_D, ssd_norm_g, ssd_w_out, attn_w_qkv, attn_sink, attn_w_out, rel_bias):
    depth = norm_g.shape[0]
    ffn = [[(ffn_w_gate[i, s].astype(MXU_DTYPE), ffn_w_up[i, s].astype(MXU_DTYPE),
             ffn_w_down[i, s].astype(MXU_DTYPE)) for s in range(2)] for i in range(depth)]
    mixers = []
    for i in range(depth):
        j = i // 2
        if i % 2 == 0:
            mixers.append(_prep_ssd(ssd_w_in[j], ssd_conv_w[j], ssd_conv_b[j], ssd_dt_bias[j], ssd_A_log[j],
                                    ssd_D[j], ssd_norm_g[j], ssd_w_out[j]))
        else:
            mixers.append(_prep_attn(attn_w_qkv[j], attn_sink[j], attn_w_out[j]))
    return tuple(_encoder(x, norm_g, ffn, mixers, rel_bias) for x in (x_prompt, x_sample))
```

```python
import functools

import jax
import jax.numpy as jnp
import numpy as np
from jax import lax
from jax.experimental import pallas as pl
from jax.experimental.pallas import tpu as pltpu

F32 = jnp.float32
MXU_DTYPE = jnp.bfloat16

EPS = 1e-6
SSD_HEAD_DIM = 64
N_GROUPS = 8
D_STATE = 128
CHUNK = 128
D_CONV = 5
CONV_PAD = D_CONV // 2
HALO = 8
BIAS_ROWS = 16
SSD_STEP = 4 * CHUNK
HEAD_DIM = 64
N_KV_HEADS = 4
BLOCK = 128
WINDOW = 128
N_BUCKETS = 32
FFN_TILE = 1024
FFN_SUBTILE = 256
PROJ_TILE = 512
ATTN_STEP = 8 * BLOCK
N_EDGE_VARIANTS = 4

V7X_VMEM_BYTES = 64 * 1024 * 1024
VMEM_LIMIT_BYTES = V7X_VMEM_BYTES - 8 * 1024 * 1024

NEG_INF = float("-inf")
LOG2_E = 1.4426950408889634


def _rmsnorm(x, g):
    return x * lax.rsqrt(jnp.mean(x * x, axis=-1, keepdims=True) + EPS) * g


def _silu(x):
    t = 0.5 * x
    return t + t * jnp.tanh(t)


def _mm(a, b):
    return jnp.dot(a.astype(MXU_DTYPE), b.astype(MXU_DTYPE), preferred_element_type=F32)


def _mm_nt(a, b):
    return lax.dot_general(a.astype(MXU_DTYPE), b.astype(MXU_DTYPE), (((1,), (1,)), ((), ())),
                           preferred_element_type=F32)


def _mm_f32(a, b):
    return jnp.dot(a, b, precision=lax.Precision.HIGHEST, preferred_element_type=F32)


def _const_spec(shape):
    zeros = (0,) * len(shape)
    return pl.BlockSpec(shape, lambda *_: zeros, pipeline_mode=pl.Buffered(1))


def _params(n_axes):
    return pltpu.CompilerParams(dimension_semantics=("arbitrary",) * n_axes,
                                vmem_limit_bytes=VMEM_LIMIT_BYTES)


def _ffn_body(x_ref, gpre_ref, gpost_ref, wg_ref, wu_ref, wd_ref, o_ref):
    for s in range(x_ref.shape[0] // FFN_SUBTILE):
        rows = slice(s * FFN_SUBTILE, (s + 1) * FFN_SUBTILE)
        x = x_ref[rows, :]
        h = _rmsnorm(x, gpre_ref[...]).astype(MXU_DTYPE)
        g = jnp.dot(h, wg_ref[...], preferred_element_type=F32)
        u = jnp.dot(h, wu_ref[...], preferred_element_type=F32)
        a = (_silu(g) * u).astype(MXU_DTYPE)
        y = jnp.dot(a, wd_ref[...], preferred_element_type=F32)
        o_ref[rows, :] = x + 0.5 * _rmsnorm(y, gpost_ref[...])


def _ffn(x, g_pre, g_post, wg, wu, wd):
    m, d = x.shape
    f = wg.shape[1]
    tm = FFN_TILE
    assert m % tm == 0 and tm % FFN_SUBTILE == 0
    row = pl.BlockSpec((tm, d), lambda i: (i, 0))
    return pl.pallas_call(
        _ffn_body,
        grid=(m // tm,),
        in_specs=[row, _const_spec((1, d)), _const_spec((1, d)),
                  _const_spec((d, f)), _const_spec((d, f)), _const_spec((f, d))],
        out_specs=row,
        out_shape=jax.ShapeDtypeStruct((m, d), x.dtype),
        compiler_params=_params(1),
        name="ffn",
    )(x, g_pre, g_post, wg, wu, wd)


def _norm_proj_body(n_w, n_wt, x_ref, g_ref, *refs):
    w_refs = refs[:n_w]
    wt_refs = refs[n_w:n_w + n_wt]
    o_refs = refs[n_w + n_wt:]
    h = _rmsnorm(x_ref[...], g_ref[...]).astype(MXU_DTYPE)
    for w_ref, o_ref in zip(w_refs, o_refs[:n_w]):
        o_ref[...] = jnp.dot(h, w_ref[...], preferred_element_type=F32).astype(o_ref.dtype)
    for wt_ref, o_ref in zip(wt_refs, o_refs[n_w:]):
        o_ref[...] = _mm_nt(wt_ref[...], h).astype(o_ref.dtype)


def _norm_proj(x, g, ws, wts, dtypes=None):
    m, d = x.shape
    tm = PROJ_TILE
    assert m % tm == 0
    dtypes = dtypes or [F32] * (len(ws) + len(wts))
    row = lambda n: pl.BlockSpec((tm, n), lambda i: (i, 0))
    col = lambda n: pl.BlockSpec((n, tm), lambda i: (0, i))
    return pl.pallas_call(
        functools.partial(_norm_proj_body, len(ws), len(wts)),
        grid=(m // tm,),
        in_specs=[row(d), _const_spec((1, d))] + [_const_spec(w.shape) for w in ws]
                 + [_const_spec(w.shape) for w in wts],
        out_specs=[row(w.shape[1]) for w in ws] + [col(w.shape[0]) for w in wts],
        out_shape=[jax.ShapeDtypeStruct((m, w.shape[1]), t) for w, t in zip(ws, dtypes)]
                  + [jax.ShapeDtypeStruct((w.shape[0], m), t) for w, t in zip(wts, dtypes[len(ws):])],
        compiler_params=_params(1),
        name="norm_proj",
    )(x, g, *ws, *wts)


def _decay_tables(dt, dt_t, alog, alog_t, reverse):
    n_chunks = dt.shape[0] // CHUNK
    row = lax.broadcasted_iota(jnp.int32, (CHUNK, CHUNK), 0)
    col = lax.broadcasted_iota(jnp.int32, (CHUNK, CHUNK), 1)
    valid = (col >= row) if reverse else (col <= row)
    valid_t = (col <= row) if reverse else (col >= row)
    a = dt * (-LOG2_E * jnp.exp(alog))
    a_t = dt_t * (-LOG2_E * jnp.exp(alog_t))
    chunks = [slice(c * CHUNK, (c + 1) * CHUNK) for c in range(n_chunks)]
    a_cols = jnp.concatenate([a[c] for c in chunks], axis=1)
    a_rows = jnp.concatenate([a_t[:, c] for c in chunks], axis=0)
    dt_rows = jnp.concatenate([dt_t[:, c] for c in chunks], axis=0)
    acs = _mm_f32(valid.astype(F32), a_cols)
    acs_t = _mm_f32(a_rows, valid_t.astype(F32))
    tot_t = _mm_f32(a_rows, jnp.ones((CHUNK, CHUNK), F32))
    w_t = dt_rows * jnp.exp2(tot_t - acs_t)
    src_t = acs_t - jnp.log2(dt_rows)
    return valid, acs, src_t, w_t


def _scan_chunk(xs, bm, cm, tables, base, n_heads, state_ref, y_ref, reverse):
    valid, acs, src_t, w_t = tables
    length = xs.shape[0]
    heads_per_group = n_heads // N_GROUPS
    group_width = heads_per_group * SSD_HEAD_DIM
    assert group_width % 128 == 0 and 128 % SSD_HEAD_DIM == 0
    heads_per_tile = 128 // SSD_HEAD_DIM
    end = 0 if reverse else length - 1

    stack_row = lax.broadcasted_iota(jnp.int32, (heads_per_group * length, group_width), 0)
    stack_col = lax.broadcasted_iota(jnp.int32, (heads_per_group * length, group_width), 1)
    block_diag = (stack_row // length) == (stack_col // SSD_HEAD_DIM)
    head_in_tile = lax.broadcasted_iota(jnp.int32, (length, 128), 1) // SSD_HEAD_DIM

    for g in range(N_GROUPS):
        bg = bm[:, g * D_STATE:(g + 1) * D_STATE]
        cg = cm[:, g * D_STATE:(g + 1) * D_STATE].astype(MXU_DTYPE)
        bg_t = bg.T
        prev = state_ref[g]
        c_both = _mm(cg, jnp.concatenate([bg_t.astype(MXU_DTYPE), prev.astype(MXU_DTYPE)], axis=1))
        cb = c_both[:, :length]
        lhs_y, lhs_s, acs_cols = [], [], []
        for r in range(heads_per_group):
            h = base + g * heads_per_group + r
            acs_col = jnp.broadcast_to(acs[:, h:h + 1], (length, length))
            seg_dt = jnp.exp2(jnp.where(valid, acs_col - src_t[h:h + 1, :], NEG_INF))
            lhs_y.append((cb * seg_dt).astype(MXU_DTYPE))
            lhs_s.append((bg_t * w_t[h:h + 1, :]).astype(MXU_DTYPE))
            acs_cols.append(acs_col)
        lanes = slice(g * group_width, (g + 1) * group_width)
        x_g = xs[:, lanes].astype(MXU_DTYPE)
        rhs = jnp.where(block_diag, jnp.concatenate([x_g] * heads_per_group, axis=0), jnp.zeros((), MXU_DTYPE))
        lhs_y, lhs_s = jnp.concatenate(lhs_y, axis=1), jnp.concatenate(lhs_s, axis=1)
        if reverse:
            y_diag = jnp.dot(lhs_y, rhs, preferred_element_type=F32)
            new = jnp.dot(lhs_s, rhs, preferred_element_type=F32)
        else:
            out = jnp.dot(jnp.concatenate([lhs_y, lhs_s], axis=0), rhs, preferred_element_type=F32)
            y_diag, new = out[:length], out[length:]
        tiles = []
        for t in range(group_width // 128):
            tile = acs_cols[t * heads_per_tile]
            for p in range(1, heads_per_tile):
                tile = jnp.where(head_in_tile == p, acs_cols[t * heads_per_tile + p], tile)
            tiles.append(tile)
        off = jnp.exp2(jnp.concatenate(tiles, axis=1))
        y_ref[:, lanes] = y_diag + c_both[:, length:] * off
        state_ref[g] = prev * off[end:end + 1, :] + new


def _ssd_bwd_body(n_heads, xbc_ref, prev_ref, next_ref, dt_ref, dtt_ref, sel_ref, convw_ref, convb_ref,
                  bias_ref, biast_ref, alog_ref, alogt_ref,
                  act_ref, dtp_ref, dtpt_ref, y_ref, state_ref):
    i = pl.program_id(1)
    n_steps = pl.num_programs(1)
    step = n_steps - 1 - i
    d_inner = n_heads * SSD_HEAD_DIM
    gn = N_GROUPS * D_STATE

    @pl.when(i == 0)
    def _():
        state_ref[...] = jnp.zeros_like(state_ref)

    dtp = jax.nn.softplus(dt_ref[...] + bias_ref[...])
    dtp_t = jax.nn.softplus(dtt_ref[...] + biast_ref[...])
    dtp_ref[...] = dtp
    dtpt_ref[...] = dtp_t

    ext_all = jnp.concatenate([jnp.where(step > 0, prev_ref[...], 0.0), xbc_ref[...],
                               jnp.where(step < n_steps - 1, next_ref[...], 0.0)], axis=0)
    ext_all = ext_all.astype(MXU_DTYPE)
    conv_w = convw_ref[...]
    tables = _decay_tables(dtp[:, n_heads:], dtp_t[n_heads:, :], alog_ref[:, n_heads:], alogt_ref[n_heads:, :],
                           reverse=True)
    for s in reversed(range(xbc_ref.shape[0] // CHUNK)):
        rows = slice(s * CHUNK, (s + 1) * CHUNK)
        ext = ext_all[s * CHUNK:(s + 1) * CHUNK + 2 * HALO]
        taps = jnp.concatenate([ext * conv_w[k:k + 1, :] for k in range(D_CONV)] + [convb_ref[...]], axis=0)
        half = jnp.dot(sel_ref[...], taps, preferred_element_type=F32)
        act = half + half * jnp.tanh(half)
        act_ref[rows, :] = act
        _scan_chunk(act[:, :d_inner], act[:, d_inner:d_inner + gn], act[:, d_inner + gn:],
                    tables, s * n_heads, n_heads, state_ref, y_ref.at[rows], reverse=True)


def _conv_select():
    ext = CHUNK + 2 * HALO
    t = np.arange(CHUNK)[:, None]
    j = np.arange(D_CONV * ext + BIAS_ROWS)[None, :]
    taps = (j < D_CONV * ext) & ((j % ext) == t + HALO - CONV_PAD + j // ext)
    bias = (j >= D_CONV * ext) & (j < D_CONV * ext + 2)
    return jnp.asarray(taps | bias, dtype=MXU_DTYPE)


def _conv_bias_rows(conv_b):
    half = 0.5 * conv_b
    hi = half.astype(MXU_DTYPE)
    lo = (half - hi.astype(F32)).astype(MXU_DTYPE)
    return jnp.concatenate([hi[None], lo[None], jnp.zeros((BIAS_ROWS - 2, conv_b.shape[0]), MXU_DTYPE)], axis=0)


def _ssd_bwd(xbc, dt_raw, dt_raw_t, conv_w, conv_b, dt_bias, dt_bias_t, alog, alog_t, batch, n_heads):
    m, c = xbc.shape
    seq = m // batch
    assert seq % SSD_STEP == 0
    nc = seq // SSD_STEP
    hb = SSD_STEP // HALO
    d_inner = n_heads * SSD_HEAD_DIM
    sel = _conv_select()
    rev = lambda b, i: b * nc + (nc - 1 - i)
    row = lambda n: pl.BlockSpec((SSD_STEP, n), lambda b, i: (rev(b, i), 0))
    col = lambda n: pl.BlockSpec((n, SSD_STEP), lambda b, i: (0, rev(b, i)))
    prev = pl.BlockSpec((HALO, c), lambda b, i: (jnp.maximum(rev(b, i) * hb - 1, 0), 0))
    nxt = pl.BlockSpec((HALO, c), lambda b, i: (jnp.minimum((rev(b, i) + 1) * hb, m // HALO - 1), 0))
    return pl.pallas_call(
        functools.partial(_ssd_bwd_body, n_heads),
        grid=(batch, nc),
        in_specs=[row(c), prev, nxt, row(2 * n_heads), col(2 * n_heads),
                  _const_spec(sel.shape), _const_spec(conv_w.shape), _const_spec(conv_b.shape),
                  _const_spec(dt_bias.shape), _const_spec(dt_bias_t.shape),
                  _const_spec(alog.shape), _const_spec(alog_t.shape)],
        out_specs=[row(c), row(2 * n_heads), col(2 * n_heads), row(d_inner)],
        out_shape=[jax.ShapeDtypeStruct((m, c), F32),
                   jax.ShapeDtypeStruct((m, 2 * n_heads), F32),
                   jax.ShapeDtypeStruct((2 * n_heads, m), F32),
                   jax.ShapeDtypeStruct((m, d_inner), F32)],
        scratch_shapes=[pltpu.VMEM((N_GROUPS, D_STATE, d_inner // N_GROUPS), F32)],
        compiler_params=_params(2),
        name="ssd_bwd",
    )(xbc, xbc, xbc, dt_raw, dt_raw_t, sel, conv_w, conv_b, dt_bias, dt_bias_t, alog, alog_t)


def _ssd_fwd_body(n_heads, act_ref, dtp_ref, dtpt_ref, ybwd_ref, z_ref, x_ref, alog_ref, alogt_ref,
                  dskip_ref, normg_ref, wout_ref, gpost_ref, o_ref, y_ref, state_ref):
    d_inner = n_heads * SSD_HEAD_DIM
    gn = N_GROUPS * D_STATE
    group_width = d_inner // N_GROUPS

    @pl.when(pl.program_id(1) == 0)
    def _():
        state_ref[...] = jnp.zeros_like(state_ref)

    tables = _decay_tables(dtp_ref[:, :n_heads], dtpt_ref[:n_heads, :], alog_ref[:, :n_heads],
                           alogt_ref[:n_heads, :], reverse=False)
    for s in range(act_ref.shape[0] // CHUNK):
        rows = slice(s * CHUNK, (s + 1) * CHUNK)
        _scan_chunk(act_ref[rows, :d_inner], act_ref[rows, d_inner:d_inner + gn], act_ref[rows, d_inner + gn:],
                    tables, s * n_heads, n_heads, state_ref, y_ref.at[rows], reverse=False)

    zh = z_ref[...]
    y = (y_ref[...] + ybwd_ref[...] + act_ref[:, :d_inner] * dskip_ref[...]) * (zh + zh * jnp.tanh(zh))
    for g in range(N_GROUPS):
        lanes = slice(g * group_width, (g + 1) * group_width)
        yg = y[:, lanes]
        yg = yg * lax.rsqrt(jnp.mean(yg * yg, axis=-1, keepdims=True) + EPS)
        y_ref[:, lanes] = yg * normg_ref[:, lanes]
    mix = jnp.dot(y_ref[...].astype(MXU_DTYPE), wout_ref[...], preferred_element_type=F32)
    o_ref[...] = x_ref[...] + _rmsnorm(mix, gpost_ref[...])


def _ssd_fwd(act, dtp, dtp_t, y_bwd, z, x, alog, alog_t, d_skip, norm_g, w_out, g_post, batch, n_heads):
    m, c = act.shape
    d = x.shape[1]
    seq = m // batch
    assert seq % SSD_STEP == 0
    nc = seq // SSD_STEP
    d_inner = n_heads * SSD_HEAD_DIM
    row = lambda n: pl.BlockSpec((SSD_STEP, n), lambda b, i: (b * nc + i, 0))
    col = lambda n: pl.BlockSpec((n, SSD_STEP), lambda b, i: (0, b * nc + i))
    return pl.pallas_call(
        functools.partial(_ssd_fwd_body, n_heads),
        grid=(batch, nc),
        in_specs=[row(c), row(2 * n_heads), col(2 * n_heads), row(d_inner), row(d_inner), row(d),
                  _const_spec(alog.shape), _const_spec(alog_t.shape), _const_spec(d_skip.shape),
                  _const_spec(norm_g.shape), _const_spec(w_out.shape), _const_spec(g_post.shape)],
        out_specs=row(d),
        out_shape=jax.ShapeDtypeStruct((m, d), F32),
        scratch_shapes=[pltpu.VMEM((SSD_STEP, d_inner), F32),
                        pltpu.VMEM((N_GROUPS, D_STATE, d_inner // N_GROUPS), F32)],
        compiler_params=_params(2),
        name="ssd_fwd",
    )(act, dtp, dtp_t, y_bwd, z, x, alog, alog_t, d_skip, norm_g, w_out, g_post)


def _t5_bias_table(relb_ref, bias_ref, group):
    kj = lax.broadcasted_iota(jnp.int32, (3 * BLOCK, BLOCK), 0)
    qi = lax.broadcasted_iota(jnp.int32, (3 * BLOCK, BLOCK), 1)
    rel = kj - BLOCK - qi
    n = jnp.abs(rel)
    half = N_BUCKETS // 2
    max_exact = half // 2
    large = jnp.full_like(n, max_exact)
    for k in range(1, half - max_exact):
        large = large + (n * n >= (max_exact * max_exact) << k).astype(jnp.int32)
    bucket = jnp.where(rel > 0, half, 0) + jnp.where(n < max_exact, n, large)
    in_window = n <= WINDOW
    n_kv = bias_ref.shape[0] // N_EDGE_VARIANTS
    for kv in range(n_kv):
        for r in range(group):
            acc = jnp.zeros((3 * BLOCK, BLOCK), F32)
            for b in range(N_BUCKETS):
                acc = jnp.where(bucket == b, relb_ref[b, kv * group + r] * LOG2_E, acc)
            acc = jnp.where(in_window, acc, NEG_INF)
            for variant in range(N_EDGE_VARIANTS):
                tbl = acc
                if variant & 1:
                    tbl = jnp.where(kj >= BLOCK, tbl, NEG_INF)
                if variant & 2:
                    tbl = jnp.where(kj < 2 * BLOCK, tbl, NEG_INF)
                bias_ref[variant * n_kv + kv, :, r * BLOCK:(r + 1) * BLOCK] = tbl


def _attn_body(qt_ref, kp_ref, kc_ref, kn_ref, vtp_ref, vtc_ref, vtn_ref, x_ref, sink_ref, relb_ref,
               wout_ref, gpost_ref, o_ref, bias_ref, ctx_ref):
    step = pl.program_id(1)
    blocks_per_step = qt_ref.shape[1] // BLOCK
    last_block = pl.num_programs(1) * blocks_per_step - 1
    n_kv = bias_ref.shape[0] // N_EDGE_VARIANTS
    group = qt_ref.shape[0] // HEAD_DIM // n_kv
    width = group * BLOCK

    @pl.when((pl.program_id(0) == 0) & (step == 0))
    def _():
        _t5_bias_table(relb_ref, bias_ref, group)

    kb_all = jnp.concatenate([kp_ref[...], kc_ref[...], kn_ref[...]], axis=0)
    vtb_all = jnp.concatenate([vtp_ref[...], vtc_ref[...], vtn_ref[...]], axis=1)
    for sub in range(blocks_per_step):
        blk = step * blocks_per_step + sub
        variant = (blk == 0).astype(jnp.int32) + 2 * (blk == last_block).astype(jnp.int32)
        kb = kb_all[sub * BLOCK:(sub + 3) * BLOCK]
        vtb = vtb_all[:, sub * BLOCK:(sub + 3) * BLOCK]
        q_lanes = slice(sub * BLOCK, (sub + 1) * BLOCK)
        q_cols = []
        for kv in range(n_kv):
            q_g = jnp.concatenate([qt_ref[(kv * group + r) * HEAD_DIM:(kv * group + r + 1) * HEAD_DIM, q_lanes]
                                   for r in range(group)], axis=1)
            q_cols.append(jnp.concatenate([q_g if p == kv else jnp.zeros_like(q_g) for p in range(n_kv)],
                                          axis=0))
        scores_all = jnp.dot(kb, jnp.concatenate(q_cols, axis=1), preferred_element_type=F32)
        for kv in range(n_kv):
            scores = scores_all[:, kv * width:(kv + 1) * width] + bias_ref[variant * n_kv + kv]
            sink = sink_ref[:, kv * width:(kv + 1) * width]
            mx = jnp.maximum(jnp.max(scores, axis=0, keepdims=True), sink)
            e = jnp.exp2(scores - mx)
            denom = jnp.sum(e, axis=0, keepdims=True) + jnp.exp2(sink - mx)
            vt_h = vtb[kv * HEAD_DIM:(kv + 1) * HEAD_DIM, :]
            ctx_t = jnp.dot(vt_h, e.astype(MXU_DTYPE), preferred_element_type=F32) / denom
            for r in range(group):
                h = kv * group + r
                ctx_ref[h * HEAD_DIM:(h + 1) * HEAD_DIM, q_lanes] = ctx_t[:, r * BLOCK:(r + 1) * BLOCK]
    ctx = ctx_ref[...].T
    mix = jnp.dot(ctx.astype(MXU_DTYPE), wout_ref[...], preferred_element_type=F32)
    o_ref[...] = x_ref[...] + _rmsnorm(mix, gpost_ref[...])


def _attn(q_t, k, v_t, x, sink_row, rel_bias, w_out, g_post, batch):
    qd, m = q_t.shape
    kvd = k.shape[1]
    d = x.shape[1]
    n_kv = kvd // HEAD_DIM
    group = qd // kvd
    assert (m // batch) % ATTN_STEP == 0
    nb = m // batch // BLOCK
    bps = ATTN_STEP // BLOCK
    ns = nb // bps
    cur_i = lambda b, i: b * ns + i
    prev_i = lambda b, i: b * nb + jnp.maximum(i * bps - 1, 0)
    next_i = lambda b, i: b * nb + jnp.minimum((i + 1) * bps, nb - 1)
    rows = lambda size, n, f: pl.BlockSpec((size, n), lambda b, i: (f(b, i), 0))
    cols = lambda size, n, f: pl.BlockSpec((n, size), lambda b, i: (0, f(b, i)))
    smem = pl.BlockSpec(memory_space=pltpu.SMEM)
    return pl.pallas_call(
        _attn_body,
        grid=(batch, ns),
        in_specs=[cols(ATTN_STEP, qd, cur_i),
                  rows(BLOCK, kvd, prev_i), rows(ATTN_STEP, kvd, cur_i), rows(BLOCK, kvd, next_i),
                  cols(BLOCK, kvd, prev_i), cols(ATTN_STEP, kvd, cur_i), cols(BLOCK, kvd, next_i),
                  rows(ATTN_STEP, d, cur_i),
                  _const_spec(sink_row.shape), smem, _const_spec(w_out.shape), _const_spec(g_post.shape)],
        out_specs=rows(ATTN_STEP, d, cur_i),
        out_shape=jax.ShapeDtypeStruct((m, d), F32),
        scratch_shapes=[pltpu.VMEM((N_EDGE_VARIANTS * n_kv, 3 * BLOCK, group * BLOCK), F32),
                        pltpu.VMEM((qd, ATTN_STEP), F32)],
        compiler_params=_params(2),
        name="attn",
    )(q_t, k, k, k, v_t, v_t, v_t, x, sink_row, rel_bias, w_out, g_post)


def _row(v):
    return v.reshape(1, -1)


def _lane_bcast(r, width):
    return jnp.broadcast_to(r.reshape(-1, 1), (r.shape[1], width))


def _prep_ssd(w_in, conv_w, conv_b, dt_bias, a_log, d_skip, norm_g, w_out):
    n_heads = d_skip.shape[0]
    d_inner = n_heads * SSD_HEAD_DIM
    conv_dim = conv_w.shape[1]
    w_dt = w_in[:, d_inner + conv_dim:].astype(MXU_DTYPE)
    bias = dt_bias.reshape(1, -1)
    alog = a_log.reshape(1, -1)
    return dict(n_heads=n_heads, w_z=(0.5 * w_in[:, :d_inner]).astype(MXU_DTYPE),
                w_xbc=w_in[:, d_inner:d_inner + conv_dim].astype(MXU_DTYPE), w_dt=w_dt, w_dt_t=w_dt.T,
                conv_w=(0.5 * conv_w).astype(MXU_DTYPE), conv_b=_conv_bias_rows(conv_b),
                bias=bias, bias_t=_lane_bcast(bias, SSD_STEP),
                alog=alog, alog_t=_lane_bcast(alog, SSD_STEP), d_skip=_row(jnp.repeat(d_skip, SSD_HEAD_DIM)),
                norm_g=_row(norm_g), w_out=w_out.astype(MXU_DTYPE))


def _ssd_layer(x, batch, g_pre, g_post, p):
    z, xbc, dt_raw, dt_raw_t = _norm_proj(x, g_pre, [p["w_z"], p["w_xbc"], p["w_dt"]], [p["w_dt_t"]])
    act, dtp, dtp_t, y_bwd = _ssd_bwd(xbc, dt_raw, dt_raw_t, p["conv_w"], p["conv_b"], p["bias"], p["bias_t"],
                                      p["alog"], p["alog_t"], batch, p["n_heads"])
    return _ssd_fwd(act, dtp, dtp_t, y_bwd, z, x, p["alog"], p["alog_t"], p["d_skip"], p["norm_g"],
                    p["w_out"], g_post, batch, p["n_heads"])


def _prep_attn(w_qkv, sink, w_out):
    q_dim = w_out.shape[0]
    kv_dim = (w_qkv.shape[1] - q_dim) // 2
    w_q_t = (w_qkv[:, :q_dim] * (LOG2_E * HEAD_DIM ** -0.5)).T.astype(MXU_DTYPE)
    return dict(w_q_t=w_q_t, w_k=w_qkv[:, q_dim:q_dim + kv_dim].astype(MXU_DTYPE),
                w_v_t=w_qkv[:, q_dim + kv_dim:].T.astype(MXU_DTYPE),
                sink_row=_row(jnp.repeat(sink * LOG2_E, BLOCK)), w_out=w_out.astype(MXU_DTYPE))


def _attn_layer(x, batch, g_pre, g_post, p, rel_bias):
    k, q_t, v_t = _norm_proj(x, g_pre, [p["w_k"]], [p["w_q_t"], p["w_v_t"]], dtypes=[MXU_DTYPE] * 3)
    return _attn(q_t, k, v_t, x, p["sink_row"], rel_bias, p["w_out"], g_post, batch)


def _encoder(x3, norm_g, ffn, mixers, rel_bias):
    batch, seq, d = x3.shape
    x = x3.reshape(batch * seq, d)
    for i, mixer in enumerate(mixers):
        ng = [_row(norm_g[i, s]) for s in range(norm_g.shape[1])]
        x = _ffn(x, ng[0], ng[1], *ffn[i][0])
        if "w_z" in mixer:
            x = _ssd_layer(x, batch, ng[2], ng[3], mixer)
        else:
            x = _attn_layer(x, batch, ng[2], ng[3], mixer, rel_bias)
        x = _ffn(x, ng[4], ng[5], *ffn[i][1])
    return x.reshape(batch, seq, d)


def kernel(x_prompt, x_sample, norm_g, ffn_w_gate, ffn_w_up, ffn_w_down, ssd_w_in, ssd_conv_w, ssd_conv_b,
           ssd_dt_bias, ssd_A_log, ssd_D, ssd_norm_g, ssd_w_out, attn_w_qkv, attn_sink, attn_w_out, rel_bias):
    depth = norm_g.shape[0]
    ffn = [[(ffn_w_gate[i, s].astype(MXU_DTYPE), ffn_w_up[i, s].astype(MXU_DTYPE),
             ffn_w_down[i, s].astype(MXU_DTYPE)) for s in range(2)] for i in range(depth)]
    mixers = []
    for i in range(depth):
        j = i // 2
        if i % 2 == 0:
            mixers.append(_prep_ssd(ssd_w_in[j], ssd_conv_w[j], ssd_conv_b[j], ssd_dt_bias[j], ssd_A_log[j],
                                    ssd_D[j], ssd_norm_g[j], ssd_w_out[j]))
        else:
            mixers.append(_prep_attn(attn_w_qkv[j], attn_sink[j], attn_w_out[j]))
    return tuple(_encoder(x, norm_g, ffn, mixers, rel_bias) for x in (x_prompt, x_sample))
```
